```python
import math
import jax, jax.numpy as jnp
from jax import lax
import numpy as np

D_MODEL = 1024
BATCH = 4
SEQ = 4096
DEPTH = 2
DEC_BATCH = 2
DEC_SEQ = 8192
PAST_LEN = 128

RET_HEADS = 4
RET_HEAD_DIM = 128
RET_DIM = RET_HEADS * RET_HEAD_DIM
RET_CHUNK = 128
CONV_DIM = D_MODEL // 2
CONV_WIDTH = 31
CONV_PAD = CONV_WIDTH // 2
AB_IN_DIM = 4 * RET_DIM + 2 * CONV_DIM
AB_OUT_DIM = RET_DIM + CONV_DIM
DIFF_HEADS = 8
DIFF_QK_DIM = 64
DIFF_V_DIM = 2 * DIFF_QK_DIM
DIFF_QK_TOTAL = DIFF_HEADS * 2 * DIFF_QK_DIM
DIFF_V_TOTAL = DIFF_HEADS * DIFF_V_DIM
C_IN_DIM = 2 * DIFF_QK_TOTAL + DIFF_V_TOTAL
Q_BLOCK = 128
D_FF = 2816
ROPE_THETA = 10000.0
EPS = 1e-6
N_EVEN = (DEPTH + 1) // 2
N_ODD = DEPTH // 2

kernel_name = 'hybrid_retention_conv_diffattn_encoder'


def lambda_init_for(layer):
    return 0.8 - 0.6 * math.exp(-0.3 * layer)


def rms_norm(x, g):
    xf = x.astype(jnp.float32)
    y = xf * lax.rsqrt(jnp.mean(xf * xf, axis=-1, keepdims=True) + EPS)
    return (y * g.astype(jnp.float32)).astype(x.dtype)


def rope(x):
    s, d = x.shape[1], x.shape[-1]
    half = d // 2
    inv = ROPE_THETA ** (-jnp.arange(half, dtype=jnp.float32) / half)
    ang = jnp.arange(s, dtype=jnp.float32)[:, None] * inv[None, :]
    cos = jnp.cos(ang)[None, :, None, :]
    sin = jnp.sin(ang)[None, :, None, :]
    xf = x.astype(jnp.float32)
    x1, x2 = xf[..., :half], xf[..., half:]
    return jnp.concatenate([x1 * cos - x2 * sin, x2 * cos + x1 * sin], axis=-1).astype(x.dtype)


def swiglu(x, w_in, w_out):
    g, u = jnp.split(x @ w_in, 2, axis=-1)
    return (jax.nn.silu(g) * u) @ w_out


def retention_one_way(q, k, v, log_gamma, strict):
    b, h, s, d = q.shape
    c = RET_CHUNK
    nc = s // c
    qc = q.reshape(b, h, nc, c, d)
    kc = k.reshape(b, h, nc, c, d)
    vc = v.reshape(b, h, nc, c, v.shape[-1])
    idx = jnp.arange(c, dtype=jnp.float32)
    diff = idx[:, None] - idx[None, :]
    mask = diff > 0 if strict else diff >= 0
    dmat = jnp.where(mask[None], jnp.exp(log_gamma[:, None, None] * jnp.maximum(diff, 0.0)[None]), 0.0)
    scores = jnp.einsum('bhnid,bhnjd->bhnij', qc, kc) * dmat[None, :, None].astype(q.dtype)
    intra = jnp.einsum('bhnij,bhnje->bhnie', scores, vc)
    k_decay = jnp.exp(log_gamma[:, None] * (c - 1 - idx)[None]).astype(k.dtype)
    kv = jnp.einsum('bhnjd,bhnje->nbhde', kc * k_decay[None, :, None, :, None], vc)
    chunk_decay = jnp.exp(log_gamma * c).astype(kv.dtype)[None, :, None, None]

    def step(state, kv_n):
        return state * chunk_decay + kv_n, state

    _, prev = lax.scan(step, jnp.zeros(kv.shape[1:], kv.dtype), kv)
    q_decay = jnp.exp(log_gamma[:, None] * (idx + 1)[None]).astype(q.dtype)
    inter = jnp.einsum('bhnid,nbhde->bhnie', qc * q_decay[None, :, None, :, None], prev)
    return (intra + inter).reshape(b, h, s, v.shape[-1])


def ab_mixer(hn, w_in, decay, ret_norm_g, conv_w, conv_b, conv_norm_g, w_out):
    b, s, _ = hn.shape
    q, k, v, g, ca, cg = jnp.split(hn @ w_in, [RET_DIM, 2 * RET_DIM, 3 * RET_DIM, 4 * RET_DIM, 4 * RET_DIM + CONV_DIM], axis=-1)
    q = rope(q.reshape(b, s, RET_HEADS, RET_HEAD_DIM)).transpose(0, 2, 1, 3)
    k = (rope(k.reshape(b, s, RET_HEADS, RET_HEAD_DIM)) * RET_HEAD_DIM ** -0.5).transpose(0, 2, 1, 3)
    v = v.reshape(b, s, RET_HEADS, RET_HEAD_DIM).transpose(0, 2, 1, 3)
    log_gamma = -jnp.exp(decay.astype(jnp.float32))
    fwd = retention_one_way(q, k, v, log_gamma[0], False)
    bwd = jnp.flip(retention_one_way(jnp.flip(q, 2), jnp.flip(k, 2), jnp.flip(v, 2), log_gamma[1], True), 2)
    o = (fwd + bwd).transpose(0, 2, 1, 3)
    o = rms_norm(o, ret_norm_g.reshape(RET_HEADS, RET_HEAD_DIM)).reshape(b, s, RET_DIM)
    ret_out = jax.nn.silu(g) * o
    u = ca * jax.nn.sigmoid(cg)
    u = lax.conv_general_dilated(u, conv_w[:, None, :], window_strides=(1,), padding=[(CONV_PAD, CONV_PAD)],
                                 dimension_numbers=('NWC', 'WIO', 'NWC'), feature_group_count=CONV_DIM) + conv_b
    u = jax.nn.silu(rms_norm(u, conv_norm_g))
    return jnp.concatenate([ret_out, u], axis=-1) @ w_out


def diff_attn_mixer(hn, w_in, q_norm_g, k_norm_g, lam, subln_g, w_out, lambda_init):
    b, s, _ = hn.shape
    q, k, v = jnp.split(hn @ w_in, [DIFF_QK_TOTAL, 2 * DIFF_QK_TOTAL], axis=-1)
    q = rope(rms_norm(q.reshape(b, s, 2 * DIFF_HEADS, DIFF_QK_DIM), q_norm_g)) * DIFF_QK_DIM ** -0.5
    k = rope(rms_norm(k.reshape(b, s, 2 * DIFF_HEADS, DIFF_QK_DIM), k_norm_g))
    q = q.reshape(b, s, DIFF_HEADS, 2, DIFF_QK_DIM)
    k = k.reshape(b, s, DIFF_HEADS, 2, DIFF_QK_DIM)
    v = v.reshape(b, s, DIFF_HEADS, DIFF_V_DIM)
    lf = lam.astype(jnp.float32)
    lam_full = jnp.exp(jnp.sum(lf[0] * lf[1])) - jnp.exp(jnp.sum(lf[2] * lf[3])) + lambda_init
    nq = s // Q_BLOCK
    qb = q.reshape(b, nq, Q_BLOCK, DIFF_HEADS, 2, DIFF_QK_DIM).transpose(1, 0, 2, 3, 4, 5)

    def block(q_blk):
        sc = jnp.einsum('bqhcd,bkhcd->bhcqk', q_blk, k).astype(jnp.float32)
        p = jax.nn.softmax(sc, axis=-1)
        w = (p[:, :, 0] - lam_full * p[:, :, 1]).astype(v.dtype)
        return jnp.einsum('bhqk,bkhe->bqhe', w, v)

    o = lax.map(block, qb)
    o = o.transpose(1, 0, 2, 3, 4).reshape(b, s, DIFF_HEADS, DIFF_V_DIM)
    o = rms_norm(o, subln_g) * (1.0 - lambda_init)
    return o.reshape(b, s, DIFF_V_TOTAL) @ w_out


def trunk(x, norm_g, ffn_w_in, ffn_w_out, ab_w_in, ab_decay, ab_ret_norm_g, ab_conv_w, ab_conv_b,
          ab_conv_norm_g, ab_w_out, c_w_in, c_q_norm_g, c_k_norm_g, c_lambda, c_subln_g, c_w_out):
    for layer in range(DEPTH):
        x = x + 0.5 * swiglu(rms_norm(x, norm_g[layer, 0]), ffn_w_in[layer, 0], ffn_w_out[layer, 0])
        hn = rms_norm(x, norm_g[layer, 1])
        i = layer // 2
        if layer % 2 == 0:
            x = x + ab_mixer(hn, ab_w_in[i], ab_decay[i], ab_ret_norm_g[i], ab_conv_w[i], ab_conv_b[i],
                             ab_conv_norm_g[i], ab_w_out[i])
        else:
            x = x + diff_attn_mixer(hn, c_w_in[i], c_q_norm_g[i], c_k_norm_g[i], c_lambda[i], c_subln_g[i],
                                    c_w_out[i], lambda_init_for(layer))
        x = x + 0.5 * swiglu(rms_norm(x, norm_g[layer, 2]), ffn_w_in[layer, 1], ffn_w_out[layer, 1])
    return x


def setup_inputs(seed: int = 0) -> dict:
    key = jax.random.key(seed)
    ks = jax.random.split(key, 20)
    f32 = jnp.float32
    nrm = lambda k, shape, scale: jax.random.normal(k, shape, f32) * scale
    decay0 = np.log(-np.log(1.0 - 2.0 ** (-5.0 - np.arange(RET_HEADS)))).astype(np.float32)
    return {
        'x_prompt': nrm(ks[0], (BATCH, SEQ, D_MODEL), 1.0),
        'x_sample': nrm(ks[1], (DEC_BATCH, DEC_SEQ, D_MODEL), 1.0),
        'norm_g': 1.0 + nrm(ks[2], (DEPTH, 3, D_MODEL), 0.02),
        'ffn_w_in': nrm(ks[3], (DEPTH, 2, D_MODEL, 2 * D_FF), D_MODEL ** -0.5),
        'ffn_w_out': nrm(ks[4], (DEPTH, 2, D_FF, D_MODEL), D_FF ** -0.5),
        'ab_w_in': nrm(ks[5], (N_EVEN, D_MODEL, AB_IN_DIM), D_MODEL ** -0.5),
        'ab_decay': jnp.asarray(decay0)[None, None, :] + nrm(ks[6], (N_EVEN, 2, RET_HEADS), 0.05),
        'ab_ret_norm_g': 1.0 + nrm(ks[7], (N_EVEN, RET_DIM), 0.02),
        'ab_conv_w': nrm(ks[8], (N_EVEN, CONV_WIDTH, CONV_DIM), CONV_WIDTH ** -0.5),
        'ab_conv_b': nrm(ks[9], (N_EVEN, CONV_DIM), 0.01),
        'ab_conv_norm_g': 1.0 + nrm(ks[10], (N_EVEN, CONV_DIM), 0.02),
        'ab_w_out': nrm(ks[11], (N_EVEN, AB_OUT_DIM, D_MODEL), AB_OUT_DIM ** -0.5),
        'c_w_in': nrm(ks[12], (N_ODD, D_MODEL, C_IN_DIM), D_MODEL ** -0.5),
        'c_q_norm_g': 1.0 + nrm(ks[13], (N_ODD, DIFF_QK_DIM), 0.02),
        'c_k_norm_g': 1.0 + nrm(ks[14], (N_ODD, DIFF_QK_DIM), 0.02),
        'c_lambda': nrm(ks[15], (N_ODD, 4, DIFF_QK_DIM), 0.1),
        'c_subln_g': 1.0 + nrm(ks[16], (N_ODD, DIFF_V_DIM), 0.02),
        'c_w_out': nrm(ks[17], (N_ODD, DIFF_V_TOTAL, D_MODEL), DIFF_V_TOTAL ** -0.5),
    }


def reference(x_prompt, x_sample, norm_g, ffn_w_in, ffn_w_out, ab_w_in, ab_decay, ab_ret_norm_g, ab_conv_w,
              ab_conv_b, ab_conv_norm_g, ab_w_out, c_w_in, c_q_norm_g, c_k_norm_g, c_lambda, c_subln_g, c_w_out):
    y_prompt = trunk(x_prompt, norm_g, ffn_w_in, ffn_w_out, ab_w_in, ab_decay, ab_ret_norm_g, ab_conv_w, ab_conv_b,
                     ab_conv_norm_g, ab_w_out, c_w_in, c_q_norm_g, c_k_norm_g, c_lambda, c_subln_g, c_w_out)
    y_sample = trunk(x_sample, norm_g, ffn_w_in, ffn_w_out, ab_w_in, ab_decay, ab_ret_norm_g, ab_conv_w, ab_conv_b,
                     ab_conv_norm_g, ab_w_out, c_w_in, c_q_norm_g, c_k_norm_g, c_lambda, c_subln_g, c_w_out)
    return (y_prompt, y_sample)
```

```python
import functools
import math

import jax
import jax.numpy as jnp
from jax import lax
from jax.experimental import pallas as pl
from jax.experimental.pallas import tpu as pltpu

F32 = jnp.float32
BF16 = jnp.bfloat16

EPS = 1e-6
ROPE_THETA = 10000.0
RET_HEADS = 4
RET_HEAD_DIM = 128
RET_DIM = RET_HEADS * RET_HEAD_DIM
RET_CHUNK = 128
CONV_DIM = 512
CONV_WIDTH = 31
CONV_PAD = CONV_WIDTH // 2
CONV_HALO = 16
DIFF_HEADS = 8
DIFF_QK_DIM = 64
DIFF_V_DIM = 128
DIFF_TOTAL = DIFF_HEADS * DIFF_V_DIM
LANES = 128

V7X_VMEM_BYTES = 64 * 1024 * 1024
VMEM_LIMIT = V7X_VMEM_BYTES - 8 * 1024 * 1024

FFN_TM = 512
FFN_TF = 256
PROJ_TM = 512
RET_TS = 512
CONV_TS = 512
CONV_RB = 64
ATT_TQ = 256
ATT_TK = 512


def _tile(n, pref):
    t = min(n, pref)
    while n % t:
        t //= 2
    return t


def _params(*sem):
    return pltpu.CompilerParams(dimension_semantics=sem, vmem_limit_bytes=VMEM_LIMIT)


def _resident(shape):
    zeros = (0,) * len(shape)
    return pl.BlockSpec(shape, lambda *_: zeros, pipeline_mode=pl.Buffered(1))


def _rms(x, g):
    return x * lax.rsqrt(jnp.mean(x * x, axis=-1, keepdims=True) + EPS) * g


def _silu(x):
    return x * jax.nn.sigmoid(x)


def _mm(a, b):
    return jnp.dot(a, b, preferred_element_type=F32)


def _ffn_kernel(x_ref, g_ref, wg_ref, wu_ref, wo_ref, o_ref, xn_ref, acc_ref, *, nf):
    x = x_ref[...]
    xn_ref[...] = _rms(x, g_ref[...]).astype(BF16)
    acc_ref[...] = jnp.zeros_like(acc_ref)
    for j in range(nf):
        xn = xn_ref[...]
        h = _silu(_mm(xn, wg_ref[j])) * _mm(xn, wu_ref[j])
        acc_ref[...] += _mm(h.astype(BF16), wo_ref[j])
    o_ref[...] = x + 0.5 * acc_ref[...]


def _ffn(x, g, wg, wu, wo):
    t, d = x.shape
    nf, _, tf = wg.shape
    tm = _tile(t, FFN_TM)
    return pl.pallas_call(
        functools.partial(_ffn_kernel, nf=nf),
        grid=(t // tm,),
        in_specs=[
            pl.BlockSpec((tm, d), lambda i: (i, 0)),
            _resident((1, d)),
            _resident((nf, d, tf)),
            _resident((nf, d, tf)),
            _resident((nf, tf, d)),
        ],
        out_specs=pl.BlockSpec((tm, d), lambda i: (i, 0)),
        out_shape=jax.ShapeDtypeStruct((t, d), F32),
        scratch_shapes=[pltpu.VMEM((tm, d), BF16), pltpu.VMEM((tm, d), F32)],
        compiler_params=_params("parallel"),
        name="ffn",
    )(x, g, wg, wu, wo)


def _ab_in_kernel(x_ref, g_ref, w_ref, cos_ref, sin_ref, q_ref, k_ref, v_ref, gate_ref, u_ref):
    xn = _rms(x_ref[...], g_ref[...]).astype(BF16)
    cos = cos_ref[...]
    sin = sin_ref[...]

    def proj(seg):
        return _mm(xn, w_ref[:, seg * RET_DIM:(seg + 1) * RET_DIM])

    def rope(y):
        return y * cos + pltpu.roll(y, RET_HEAD_DIM // 2, 1) * sin

    q = proj(0)
    k = proj(1)
    for h in range(RET_HEADS):
        cols = slice(h * RET_HEAD_DIM, (h + 1) * RET_HEAD_DIM)
        q_ref[:, cols] = rope(q[:, cols])
        k_ref[:, cols] = rope(k[:, cols]) * RET_HEAD_DIM ** -0.5
    v_ref[...] = proj(2).astype(BF16)
    gate_ref[...] = proj(3)
    u_ref[...] = proj(4) * jax.nn.sigmoid(proj(5))


def _ab_in(x, g, w, cos, sin, seq):
    t, d = x.shape
    tm = _tile(seq, PROJ_TM)
    npos = seq // tm
    tok = lambda i: (i, 0)
    pos = lambda i: (i % npos, 0)
    wide = jax.ShapeDtypeStruct((t, RET_DIM), F32)
    return pl.pallas_call(
        _ab_in_kernel,
        grid=(t // tm,),
        in_specs=[
            pl.BlockSpec((tm, d), tok),
            _resident((1, d)),
            _resident(w.shape),
            pl.BlockSpec((tm, RET_HEAD_DIM), pos),
            pl.BlockSpec((tm, RET_HEAD_DIM), pos),
        ],
        out_specs=[pl.BlockSpec((tm, RET_DIM), tok)] * 5,
        out_shape=[wide, wide, jax.ShapeDtypeStruct((t, RET_DIM), BF16), wide, wide],
        compiler_params=_params("parallel"),
        name="ab_in",
    )(x, g, w, cos, sin)


def _ret_decays(decay_ref, direction, h):
    c = RET_CHUNK
    lg = -jnp.exp(jnp.full((1, 1), decay_ref[direction, h], F32))
    idx = lax.broadcasted_iota(jnp.int32, (c, 1), 0).astype(F32)
    if direction == 0:
        q_dec = jnp.exp(lg * (idx + 1.0))
        k_dec = jnp.exp(lg * (c - 1.0 - idx))
    else:
        q_dec = jnp.exp(lg * (c - idx))
        k_dec = jnp.exp(lg * idx)
    return lg, q_dec, k_dec, jnp.exp(lg * float(c))


def _ret_bwd_kernel(decay_ref, q_ref, k_ref, v_ref, o_ref, state_ref, *, ts):
    c = RET_CHUNK

    @pl.when(pl.program_id(1) == 0)
    def _():
        state_ref[...] = jnp.zeros_like(state_ref)

    dec = [_ret_decays(decay_ref, 1, h) for h in range(RET_HEADS)]
    for ci in reversed(range(ts // c)):
        rows = slice(ci * c, (ci + 1) * c)
        for h in range(RET_HEADS):
            cols = slice(h * RET_HEAD_DIM, (h + 1) * RET_HEAD_DIM)
            _, q_dec, k_dec, c_dec = dec[h]
            qc = q_ref[0, rows, cols]
            kc = k_ref[0, rows, cols]
            vc = v_ref[0, rows, cols]
            state = state_ref[h]
            o_ref[0, rows, cols] = _mm((qc * q_dec).astype(BF16), state.astype(BF16))
            state_ref[h] = state * c_dec + _mm((kc * k_dec).T.astype(BF16), vc)


def _ret_fwd_kernel(decay_ref, q_ref, k_ref, v_ref, ob_ref, gate_ref, gn_ref, o_ref, state_ref, *, ts):
    c = RET_CHUNK

    @pl.when(pl.program_id(1) == 0)
    def _():
        state_ref[...] = jnp.zeros_like(state_ref)

    diff = (lax.broadcasted_iota(jnp.int32, (c, c), 0) - lax.broadcasted_iota(jnp.int32, (c, c), 1)).astype(F32)
    dec = []
    for h in range(RET_HEADS):
        lgf, q_dec, k_dec, c_dec = _ret_decays(decay_ref, 0, h)
        lgb = _ret_decays(decay_ref, 1, h)[0]
        dmat = jnp.where(diff >= 0, jnp.exp(lgf * jnp.maximum(diff, 0.0)), jnp.exp(lgb * jnp.maximum(-diff, 0.0)))
        dec.append((dmat, q_dec, k_dec, c_dec))
    for ci in range(ts // c):
        rows = slice(ci * c, (ci + 1) * c)
        for h in range(RET_HEADS):
            cols = slice(h * RET_HEAD_DIM, (h + 1) * RET_HEAD_DIM)
            dmat, q_dec, k_dec, c_dec = dec[h]
            qc = q_ref[0, rows, cols]
            kc = k_ref[0, rows, cols]
            vc = v_ref[0, rows, cols]
            state = state_ref[h]
            scores = lax.dot_general(qc.astype(BF16), kc.astype(BF16), (((1,), (1,)), ((), ())),
                                     preferred_element_type=F32) * dmat
            o = _mm(scores.astype(BF16), vc) + _mm((qc * q_dec).astype(BF16), state.astype(BF16))
            o = o + ob_ref[0, rows, cols]
            state_ref[h] = state * c_dec + _mm((kc * k_dec).T.astype(BF16), vc)
            o_ref[0, rows, cols] = (_silu(gate_ref[0, rows, cols]) * _rms(o, gn_ref[:, cols])).astype(BF16)


def _retention(decay, q, k, v, gate, gn):
    b, s, d = q.shape
    ts = _tile(s, RET_TS)
    nt = s // ts
    state = pltpu.VMEM((RET_HEADS, RET_HEAD_DIM, RET_HEAD_DIM), F32)
    fwd = lambda bi, ti: (bi, ti, 0)
    rev = lambda bi, ti: (bi, nt - 1 - ti, 0)
    smem = pl.BlockSpec(memory_space=pltpu.SMEM)
    ob = pl.pallas_call(
        functools.partial(_ret_bwd_kernel, ts=ts),
        grid=(b, nt),
        in_specs=[smem] + [pl.BlockSpec((1, ts, d), rev)] * 3,
        out_specs=pl.BlockSpec((1, ts, d), rev),
        out_shape=jax.ShapeDtypeStruct((b, s, d), F32),
        scratch_shapes=[state],
        compiler_params=_params("parallel", "arbitrary"),
        name="ret_bwd",
    )(decay, q, k, v)
    return pl.pallas_call(
        functools.partial(_ret_fwd_kernel, ts=ts),
        grid=(b, nt),
        in_specs=[smem] + [pl.BlockSpec((1, ts, d), fwd)] * 5 + [_resident(gn.shape)],
        out_specs=pl.BlockSpec((1, ts, d), fwd),
        out_shape=jax.ShapeDtypeStruct((b, s, d), BF16),
        scratch_shapes=[state],
        compiler_params=_params("parallel", "arbitrary"),
        name="ret_fwd",
    )(decay, q, k, v, ob, gate, gn)


def _conv_kernel(u_ref, prev_ref, next_ref, w_ref, b_ref, g_ref, o_ref, buf_ref, *, ts, rb):
    i = pl.program_id(1)
    zero = jnp.zeros((CONV_HALO, CONV_DIM), F32)
    buf_ref[0:CONV_HALO, :] = jnp.where(i > 0, prev_ref[0], zero)
    buf_ref[CONV_HALO:CONV_HALO + ts, :] = u_ref[0]
    buf_ref[CONV_HALO + ts:, :] = jnp.where(i < pl.num_programs(1) - 1, next_ref[0], zero)
    w = w_ref[...]
    first = CONV_HALO - CONV_PAD
    for r in range(0, ts, rb):
        acc = jnp.zeros((rb, CONV_DIM), F32)
        for t in range(CONV_WIDTH):
            acc = acc + w[t:t + 1, :] * buf_ref[r + first + t:r + first + t + rb, :]
        y = _rms(acc + b_ref[...], g_ref[...])
        o_ref[0, r:r + rb, :] = _silu(y).astype(BF16)


def _conv(u, w, bias, g):
    b, s, d = u.shape
    ts = _tile(s, CONV_TS)
    nt = s // ts
    per = ts // CONV_HALO
    nh = s // CONV_HALO
    return pl.pallas_call(
        functools.partial(_conv_kernel, ts=ts, rb=_tile(ts, CONV_RB)),
        grid=(b, nt),
        in_specs=[
            pl.BlockSpec((1, ts, d), lambda bi, ti: (bi, ti, 0)),
            pl.BlockSpec((1, CONV_HALO, d), lambda bi, ti: (bi, jnp.maximum(ti * per - 1, 0), 0)),
            pl.BlockSpec((1, CONV_HALO, d), lambda bi, ti: (bi, jnp.minimum((ti + 1) * per, nh - 1), 0)),
            _resident(w.shape),
            _resident(bias.shape),
            _resident(g.shape),
        ],
        out_specs=pl.BlockSpec((1, ts, d), lambda bi, ti: (bi, ti, 0)),
        out_shape=jax.ShapeDtypeStruct((b, s, d), BF16),
        scratch_shapes=[pltpu.VMEM((ts + 2 * CONV_HALO, d), F32)],
        compiler_params=_params("parallel", "parallel"),
        name="conv",
    )(u, u, u, w, bias, g)


def _out_kernel(x_ref, *refs, n):
    acc = x_ref[...]
    for a_ref, w_ref in zip(refs[:n], refs[n:2 * n]):
        acc = acc + _mm(a_ref[...], w_ref[...])
    refs[2 * n][...] = acc


def _out_proj(x, acts, ws):
    t, d = x.shape
    tm = _tile(t, PROJ_TM)
    n = len(acts)
    tok = lambda i: (i, 0)
    return pl.pallas_call(
        functools.partial(_out_kernel, n=n),
        grid=(t // tm,),
        in_specs=[pl.BlockSpec((tm, d), tok)]
        + [pl.BlockSpec((tm, a.shape[1]), tok) for a in acts]
        + [_resident(w.shape) for w in ws],
        out_specs=pl.BlockSpec((tm, d), tok),
        out_shape=jax.ShapeDtypeStruct((t, d), F32),
        compiler_params=_params("parallel"),
        name="out_proj",
    )(x, *acts, *ws)


def _c_in_kernel(x_ref, g_ref, w_ref, qg_ref, kg_ref, cos_ref, sin_ref, q_ref, k_ref, v_ref):
    xn = _rms(x_ref[...], g_ref[...]).astype(BF16)
    cos = cos_ref[...]
    sin = sin_ref[...]
    row = lax.broadcasted_iota(jnp.int32, (LANES, LANES), 0) >= DIFF_QK_DIM
    col = lax.broadcasted_iota(jnp.int32, (LANES, LANES), 1) >= DIFF_QK_DIM
    group = jnp.where(row == col, 1.0, 0.0).astype(BF16)
    lane = lax.broadcasted_iota(jnp.int32, (1, LANES), 1)
    low_half = jnp.bitwise_and(lane, DIFF_QK_DIM // 2) == 0
    wide = 2 * LANES

    def head_pair(y, gain, scale):
        sq = y * y
        hi = sq.astype(BF16)
        lo = (sq - hi.astype(F32)).astype(BF16)
        ms = (_mm(hi, group) + _mm(lo, group)) * (1.0 / DIFF_QK_DIM)
        yn = y * lax.rsqrt(ms + EPS) * gain
        half = DIFF_QK_DIM // 2
        partner = jnp.where(low_half, pltpu.roll(yn, LANES - half, 1), pltpu.roll(yn, half, 1))
        return ((yn * cos + partner * sin) * scale).astype(BF16)

    for part, (gain_ref, o_ref, scale) in enumerate(((qg_ref, q_ref, DIFF_QK_DIM ** -0.5), (kg_ref, k_ref, 1.0))):
        for cb in range(0, DIFF_TOTAL, wide):
            y = _mm(xn, w_ref[:, part * DIFF_TOTAL + cb:part * DIFF_TOTAL + cb + wide])
            for sub in range(0, wide, LANES):
                o_ref[:, cb + sub:cb + sub + LANES] = head_pair(y[:, sub:sub + LANES], gain_ref[...], scale)
    for cb in range(0, DIFF_TOTAL, wide):
        v_ref[:, cb:cb + wide] = _mm(xn, w_ref[:, 2 * DIFF_TOTAL + cb:2 * DIFF_TOTAL + cb + wide]).astype(BF16)


def _c_in(x, g, w, qg, kg, cos, sin, seq):
    t, d = x.shape
    tm = _tile(seq, PROJ_TM)
    npos = seq // tm
    tok = lambda i: (i, 0)
    pos = lambda i: (i % npos, 0)
    out = jax.ShapeDtypeStruct((t, DIFF_TOTAL), BF16)
    return pl.pallas_call(
        _c_in_kernel,
        grid=(t // tm,),
        in_specs=[
            pl.BlockSpec((tm, d), tok),
            _resident((1, d)),
            _resident(w.shape),
            _resident(qg.shape),
            _resident(kg.shape),
            pl.BlockSpec((tm, LANES), pos),
            pl.BlockSpec((tm, LANES), pos),
        ],
        out_specs=[pl.BlockSpec((tm, DIFF_TOTAL), tok)] * 3,
        out_shape=[out, out, out],
        compiler_params=_params("parallel"),
        name="c_in",
    )(x, g, w, qg, kg, cos, sin)


def _attn_kernel(lam_ref, sg_ref, q_ref, k_ref, v_ref, o_ref, m_ref, l_ref, acc_ref, *, tq, tk, nk, lam_init):
    q = q_ref[0].astype(F32)
    lane = lax.broadcasted_iota(jnp.int32, q.shape, 1)
    qs = jnp.concatenate([jnp.where(lane < DIFF_QK_DIM, q, 0.0), jnp.where(lane >= DIFF_QK_DIM, q, 0.0)],
                         axis=0).astype(BF16)
    m_ref[...] = jnp.full_like(m_ref, -jnp.inf)
    l_ref[...] = jnp.zeros_like(l_ref)
    acc_ref[...] = jnp.zeros_like(acc_ref)

    def body(j, carry):
        off = pl.multiple_of(j * tk, tk)
        kj = k_ref[0, pl.ds(off, tk), :]
        vj = v_ref[0, pl.ds(off, tk), :]
        s = lax.dot_general(qs, kj, (((1,), (1,)), ((), ())), preferred_element_type=F32)
        m_prev = m_ref[...]
        m_new = jnp.maximum(m_prev, jnp.max(s, axis=1, keepdims=True))
        alpha = jnp.exp(m_prev - m_new)
        p = jnp.exp(s - m_new)
        l_ref[...] = alpha * l_ref[...] + jnp.sum(p, axis=1, keepdims=True)
        acc_ref[...] = alpha * acc_ref[...] + _mm(p.astype(BF16), vj)
        m_ref[...] = m_new
        return carry

    lax.fori_loop(0, nk, body, 0)
    lam = lam_ref[...]
    lam_full = (jnp.exp(jnp.sum(lam[0:1] * lam[1:2], axis=1, keepdims=True))
                - jnp.exp(jnp.sum(lam[2:3] * lam[3:4], axis=1, keepdims=True)) + lam_init)
    out = acc_ref[...] / l_ref[...]
    o = out[:tq] - lam_full * out[tq:]
    o_ref[0] = (_rms(o, sg_ref[...]) * (1.0 - lam_init)).astype(BF16)


def _diff_attn(lam, sg, q, k, v, lam_init):
    b, s, d = q.shape
    tq = _tile(s, ATT_TQ)
    tk = _tile(s, ATT_TK)
    kv = pl.BlockSpec((1, s, DIFF_V_DIM), lambda bi, h, qi: (bi, 0, h))
    qo = pl.BlockSpec((1, tq, DIFF_V_DIM), lambda bi, h, qi: (bi, qi, h))
    return pl.pallas_call(
        functools.partial(_attn_kernel, tq=tq, tk=tk, nk=s // tk, lam_init=lam_init),
        grid=(b, DIFF_HEADS, s // tq),
        in_specs=[_resident(lam.shape), _resident(sg.shape), qo, kv, kv],
        out_specs=qo,
        out_shape=jax.ShapeDtypeStruct((b, s, d), BF16),
        scratch_shapes=[pltpu.VMEM((2 * tq, 1), F32), pltpu.VMEM((2 * tq, 1), F32),
                        pltpu.VMEM((2 * tq, DIFF_V_DIM), F32)],
        compiler_params=_params("parallel", "parallel", "arbitrary"),
        name="diff_attn",
    )(lam, sg, q, k, v)


def _rope_tables(seq, head_dim):
    half = head_dim // 2
    inv = ROPE_THETA ** (-jnp.arange(half, dtype=F32) / half)
    ang = jnp.arange(seq, dtype=F32)[:, None] * inv[None, :]
    cos = jnp.cos(ang)
    sin = jnp.sin(ang)
    reps = LANES // head_dim
    return (jnp.tile(jnp.concatenate([cos, cos], axis=-1), (1, reps)),
            jnp.tile(jnp.concatenate([-sin, sin], axis=-1), (1, reps)))


def _ffn_weights(w_in, w_out):
    d, two_ff = w_in.shape
    ff = two_ff // 2
    tf = _tile(ff, FFN_TF)
    w = w_in.astype(BF16).reshape(d, 2, ff // tf, tf).transpose(1, 2, 0, 3)
    return w[0], w[1], w_out.astype(BF16).reshape(ff // tf, tf, d)


def _lambda_init(layer):
    return 0.8 - 0.6 * math.exp(-0.3 * layer)


def kernel(x_prompt, x_sample, norm_g, ffn_w_in, ffn_w_out, ab_w_in, ab_decay, ab_ret_norm_g, ab_conv_w, ab_conv_b,
           ab_conv_norm_g, ab_w_out, c_w_in, c_q_norm_g, c_k_norm_g, c_lambda, c_subln_g, c_w_out):
    depth = norm_g.shape[0]
    d = norm_g.shape[-1]
    ffn = [[_ffn_weights(ffn_w_in[l, j], ffn_w_out[l, j]) for j in range(2)] for l in range(depth)]
    ab_w_in_b = ab_w_in.astype(BF16)
    ab_w_out_b = ab_w_out.astype(BF16)
    c_w_in_b = c_w_in.astype(BF16)
    c_w_out_b = c_w_out.astype(BF16)

    def trunk(x3):
        b, s, _ = x3.shape
        x = x3.reshape(b * s, d)
        seq3 = lambda a: a.reshape(b, s, a.shape[-1])
        flat = lambda a: a.reshape(b * s, a.shape[-1])
        for layer in range(depth):
            g = norm_g[layer][:, None, :]
            x = _ffn(x, g[0], *ffn[layer][0])
            i = layer // 2
            if layer % 2 == 0:
                cos, sin = _rope_tables(s, RET_HEAD_DIM)
                q, k, v, gate, u = _ab_in(x, g[1], ab_w_in_b[i], cos, sin, s)
                ret = _retention(ab_decay[i], seq3(q), seq3(k), seq3(v), seq3(gate), ab_ret_norm_g[i][None])
                conv = _conv(seq3(u), ab_conv_w[i], ab_conv_b[i][None], ab_conv_norm_g[i][None])
                x = _out_proj(x, [flat(ret), flat(conv)], [ab_w_out_b[i, :RET_DIM], ab_w_out_b[i, RET_DIM:]])
            else:
                cos, sin = _rope_tables(s, DIFF_QK_DIM)
                reps = LANES // DIFF_QK_DIM
                q, k, v = _c_in(x, g[1], c_w_in_b[i], jnp.tile(c_q_norm_g[i], reps)[None],
                                jnp.tile(c_k_norm_g[i], reps)[None], cos, sin, s)
                o = _diff_attn(c_lambda[i], c_subln_g[i][None], seq3(q), seq3(k), seq3(v), _lambda_init(layer))
                x = _out_proj(x, [flat(o)], [c_w_out_b[i]])
            x = _ffn(x, g[2], *ffn[layer][1])
        return x.reshape(b, s, d)

    return trunk(x_prompt), trunk(x_sample)
```

```python
import functools
import math

import jax
import jax.numpy as jnp
from jax import lax
from jax.experimental import pallas as pl
from jax.experimental.pallas import tpu as pltpu

F32 = jnp.float32
BF16 = jnp.bfloat16

EPS = 1e-6
ROPE_THETA = 10000.0
RET_HEADS = 4
RET_HEAD_DIM = 128
RET_DIM = RET_HEADS * RET_HEAD_DIM
RET_CHUNK = 128
CONV_DIM = 512
CONV_WIDTH = 31
CONV_PAD = CONV_WIDTH // 2
CONV_HALO = 16
DIFF_HEADS = 8
DIFF_QK_DIM = 64
DIFF_V_DIM = 128
DIFF_TOTAL = DIFF_HEADS * DIFF_V_DIM
LANES = 128

V7X_VMEM_BYTES = 64 * 1024 * 1024
VMEM_LIMIT = V7X_VMEM_BYTES - 8 * 1024 * 1024

FFN_TM = 512
FFN_TF = 256
PROJ_TM = 512
RET_TS = 512
CONV_TS = 512
CONV_RB = 64
ATT_TQ = 256
ATT_TK = 512
ATT_ONES_ROWS = 16
ATT_V_ROWS = DIFF_V_DIM + ATT_ONES_ROWS


def _tile(n, pref):
    t = min(n, pref)
    while n % t:
        t //= 2
    return t


def _params(*sem):
    return pltpu.CompilerParams(dimension_semantics=sem, vmem_limit_bytes=VMEM_LIMIT)


def _resident(shape):
    zeros = (0,) * len(shape)
    return pl.BlockSpec(shape, lambda *_: zeros, pipeline_mode=pl.Buffered(1))


def _rms(x, g):
    return x * lax.rsqrt(jnp.mean(x * x, axis=-1, keepdims=True) + EPS) * g


def _silu(x):
    return x * jax.nn.sigmoid(x)


def _mm(a, b):
    return jnp.dot(a, b, preferred_element_type=F32)


def _ffn_kernel(x_ref, g_ref, wg_ref, wu_ref, wo_ref, o_ref, xn_ref, acc_ref, *, nf):
    x = x_ref[...]
    xn_ref[...] = _rms(x, g_ref[...]).astype(BF16)
    acc_ref[...] = jnp.zeros_like(acc_ref)
    for j in range(nf):
        xn = xn_ref[...]
        h = _silu(_mm(xn, wg_ref[j])) * _mm(xn, wu_ref[j])
        acc_ref[...] += _mm(h.astype(BF16), wo_ref[j])
    o_ref[...] = x + 0.5 * acc_ref[...]


def _ffn(x, g, wg, wu, wo):
    t, d = x.shape
    nf, _, tf = wg.shape
    tm = _tile(t, FFN_TM)
    return pl.pallas_call(
        functools.partial(_ffn_kernel, nf=nf),
        grid=(t // tm,),
        in_specs=[
            pl.BlockSpec((tm, d), lambda i: (i, 0)),
            _resident((1, d)),
            _resident((nf, d, tf)),
            _resident((nf, d, tf)),
            _resident((nf, tf, d)),
        ],
        out_specs=pl.BlockSpec((tm, d), lambda i: (i, 0)),
        out_shape=jax.ShapeDtypeStruct((t, d), F32),
        scratch_shapes=[pltpu.VMEM((tm, d), BF16), pltpu.VMEM((tm, d), F32)],
        compiler_params=_params("parallel"),
        name="ffn",
    )(x, g, wg, wu, wo)


def _ab_in_kernel(x_ref, g_ref, w_ref, cos_ref, sin_ref, q_ref, k_ref, v_ref, gate_ref, u_ref):
    xn = _rms(x_ref[...], g_ref[...]).astype(BF16)
    cos = cos_ref[...]
    sin = sin_ref[...]

    def proj(seg):
        return _mm(xn, w_ref[:, seg * RET_DIM:(seg + 1) * RET_DIM])

    def rope(y):
        return y * cos + pltpu.roll(y, RET_HEAD_DIM // 2, 1) * sin

    q = proj(0)
    k = proj(1)
    for h in range(RET_HEADS):
        cols = slice(h * RET_HEAD_DIM, (h + 1) * RET_HEAD_DIM)
        q_ref[:, cols] = rope(q[:, cols])
        k_ref[:, cols] = rope(k[:, cols]) * RET_HEAD_DIM ** -0.5
    v_ref[...] = proj(2).astype(BF16)
    gate_ref[...] = proj(3)
    u_ref[...] = proj(4) * jax.nn.sigmoid(proj(5))


def _ab_in(x, g, w, cos, sin, seq):
    t, d = x.shape
    tm = _tile(seq, PROJ_TM)
    npos = seq // tm
    tok = lambda i: (i, 0)
    pos = lambda i: (i % npos, 0)
    wide = jax.ShapeDtypeStruct((t, RET_DIM), F32)
    return pl.pallas_call(
        _ab_in_kernel,
        grid=(t // tm,),
        in_specs=[
            pl.BlockSpec((tm, d), tok),
            _resident((1, d)),
            _resident(w.shape),
            pl.BlockSpec((tm, RET_HEAD_DIM), pos),
            pl.BlockSpec((tm, RET_HEAD_DIM), pos),
        ],
        out_specs=[pl.BlockSpec((tm, RET_DIM), tok)] * 5,
        out_shape=[wide, wide, jax.ShapeDtypeStruct((t, RET_DIM), BF16), wide, wide],
        compiler_params=_params("parallel"),
        name="ab_in",
    )(x, g, w, cos, sin)


def _ret_decays(decay_ref, direction, h):
    c = RET_CHUNK
    lg = -jnp.exp(jnp.full((1, 1), decay_ref[direction, h], F32))
    idx = lax.broadcasted_iota(jnp.int32, (c, 1), 0).astype(F32)
    if direction == 0:
        q_dec = jnp.exp(lg * (idx + 1.0))
        k_dec = jnp.exp(lg * (c - 1.0 - idx))
    else:
        q_dec = jnp.exp(lg * (c - idx))
        k_dec = jnp.exp(lg * idx)
    return lg, q_dec, k_dec, jnp.exp(lg * float(c))


def _ret_bwd_kernel(decay_ref, q_ref, k_ref, v_ref, o_ref, state_ref, *, ts):
    c = RET_CHUNK

    @pl.when(pl.program_id(1) == 0)
    def _():
        state_ref[...] = jnp.zeros_like(state_ref)

    dec = [_ret_decays(decay_ref, 1, h) for h in range(RET_HEADS)]
    for ci in reversed(range(ts // c)):
        rows = slice(ci * c, (ci + 1) * c)
        for h in range(RET_HEADS):
            cols = slice(h * RET_HEAD_DIM, (h + 1) * RET_HEAD_DIM)
            _, q_dec, k_dec, c_dec = dec[h]
            qc = q_ref[0, rows, cols]
            kc = k_ref[0, rows, cols]
            vc = v_ref[0, rows, cols]
            state = state_ref[h]
            o_ref[0, rows, cols] = _mm((qc * q_dec).astype(BF16), state.astype(BF16))
            state_ref[h] = state * c_dec + _mm((kc * k_dec).T.astype(BF16), vc)


def _ret_fwd_kernel(decay_ref, q_ref, k_ref, v_ref, ob_ref, gate_ref, gn_ref, o_ref, state_ref, *, ts):
    c = RET_CHUNK

    @pl.when(pl.program_id(1) == 0)
    def _():
        state_ref[...] = jnp.zeros_like(state_ref)

    diff = (lax.broadcasted_iota(jnp.int32, (c, c), 0) - lax.broadcasted_iota(jnp.int32, (c, c), 1)).astype(F32)
    dec = []
    for h in range(RET_HEADS):
        lgf, q_dec, k_dec, c_dec = _ret_decays(decay_ref, 0, h)
        lgb = _ret_decays(decay_ref, 1, h)[0]
        dmat = jnp.where(diff >= 0, jnp.exp(lgf * jnp.maximum(diff, 0.0)), jnp.exp(lgb * jnp.maximum(-diff, 0.0)))
        dec.append((dmat, q_dec, k_dec, c_dec))
    for ci in range(ts // c):
        rows = slice(ci * c, (ci + 1) * c)
        for h in range(RET_HEADS):
            cols = slice(h * RET_HEAD_DIM, (h + 1) * RET_HEAD_DIM)
            dmat, q_dec, k_dec, c_dec = dec[h]
            qc = q_ref[0, rows, cols]
            kc = k_ref[0, rows, cols]
            vc = v_ref[0, rows, cols]
            state = state_ref[h]
            scores = lax.dot_general(qc.astype(BF16), kc.astype(BF16), (((1,), (1,)), ((), ())),
                                     preferred_element_type=F32) * dmat
            o = _mm(scores.astype(BF16), vc) + _mm((qc * q_dec).astype(BF16), state.astype(BF16))
            o = o + ob_ref[0, rows, cols]
            state_ref[h] = state * c_dec + _mm((kc * k_dec).T.astype(BF16), vc)
            o_ref[0, rows, cols] = (_silu(gate_ref[0, rows, cols]) * _rms(o, gn_ref[:, cols])).astype(BF16)


def _retention(decay, q, k, v, gate, gn):
    b, s, d = q.shape
    ts = _tile(s, RET_TS)
    nt = s // ts
    state = pltpu.VMEM((RET_HEADS, RET_HEAD_DIM, RET_HEAD_DIM), F32)
    fwd = lambda bi, ti: (bi, ti, 0)
    rev = lambda bi, ti: (bi, nt - 1 - ti, 0)
    smem = pl.BlockSpec(memory_space=pltpu.SMEM)
    ob = pl.pallas_call(
        functools.partial(_ret_bwd_kernel, ts=ts),
        grid=(b, nt),
        in_specs=[smem] + [pl.BlockSpec((1, ts, d), rev)] * 3,
        out_specs=pl.BlockSpec((1, ts, d), rev),
        out_shape=jax.ShapeDtypeStruct((b, s, d), F32),
        scratch_shapes=[state],
        compiler_params=_params("parallel", "arbitrary"),
        name="ret_bwd",
    )(decay, q, k, v)
    return pl.pallas_call(
        functools.partial(_ret_fwd_kernel, ts=ts),
        grid=(b, nt),
        in_specs=[smem] + [pl.BlockSpec((1, ts, d), fwd)] * 5 + [_resident(gn.shape)],
        out_specs=pl.BlockSpec((1, ts, d), fwd),
        out_shape=jax.ShapeDtypeStruct((b, s, d), BF16),
        scratch_shapes=[state],
        compiler_params=_params("parallel", "arbitrary"),
        name="ret_fwd",
    )(decay, q, k, v, ob, gate, gn)


def _conv_kernel(u_ref, prev_ref, next_ref, w_ref, b_ref, g_ref, o_ref, buf_ref, *, ts, rb):
    i = pl.program_id(1)
    zero = jnp.zeros((CONV_HALO, CONV_DIM), F32)
    buf_ref[0:CONV_HALO, :] = jnp.where(i > 0, prev_ref[0], zero)
    buf_ref[CONV_HALO:CONV_HALO + ts, :] = u_ref[0]
    buf_ref[CONV_HALO + ts:, :] = jnp.where(i < pl.num_programs(1) - 1, next_ref[0], zero)
    w = w_ref[...]
    first = CONV_HALO - CONV_PAD
    for r in range(0, ts, rb):
        acc = jnp.zeros((rb, CONV_DIM), F32)
        for t in range(CONV_WIDTH):
            acc = acc + w[t:t + 1, :] * buf_ref[r + first + t:r + first + t + rb, :]
        y = _rms(acc + b_ref[...], g_ref[...])
        o_ref[0, r:r + rb, :] = _silu(y).astype(BF16)


def _conv(u, w, bias, g):
    b, s, d = u.shape
    ts = _tile(s, CONV_TS)
    nt = s // ts
    per = ts // CONV_HALO
    nh = s // CONV_HALO
    return pl.pallas_call(
        functools.partial(_conv_kernel, ts=ts, rb=_tile(ts, CONV_RB)),
        grid=(b, nt),
        in_specs=[
            pl.BlockSpec((1, ts, d), lambda bi, ti: (bi, ti, 0)),
            pl.BlockSpec((1, CONV_HALO, d), lambda bi, ti: (bi, jnp.maximum(ti * per - 1, 0), 0)),
            pl.BlockSpec((1, CONV_HALO, d), lambda bi, ti: (bi, jnp.minimum((ti + 1) * per, nh - 1), 0)),
            _resident(w.shape),
            _resident(bias.shape),
            _resident(g.shape),
        ],
        out_specs=pl.BlockSpec((1, ts, d), lambda bi, ti: (bi, ti, 0)),
        out_shape=jax.ShapeDtypeStruct((b, s, d), BF16),
        scratch_shapes=[pltpu.VMEM((ts + 2 * CONV_HALO, d), F32)],
        compiler_params=_params("parallel", "parallel"),
        name="conv",
    )(u, u, u, w, bias, g)


def _out_kernel(x_ref, *refs, n):
    acc = x_ref[...]
    for a_ref, w_ref in zip(refs[:n], refs[n:2 * n]):
        acc = acc + _mm(a_ref[...], w_ref[...])
    refs[2 * n][...] = acc


def _out_proj(x, acts, ws):
    t, d = x.shape
    tm = _tile(t, PROJ_TM)
    n = len(acts)
    tok = lambda i: (i, 0)
    return pl.pallas_call(
        functools.partial(_out_kernel, n=n),
        grid=(t // tm,),
        in_specs=[pl.BlockSpec((tm, d), tok)]
        + [pl.BlockSpec((tm, a.shape[1]), tok) for a in acts]
        + [_resident(w.shape) for w in ws],
        out_specs=pl.BlockSpec((tm, d), tok),
        out_shape=jax.ShapeDtypeStruct((t, d), F32),
        compiler_params=_params("parallel"),
        name="out_proj",
    )(x, *acts, *ws)


def _c_in_kernel(x_ref, g_ref, w_ref, qg_ref, kg_ref, cos_ref, sin_ref, qt_ref, k_ref, vt_ref):
    xn = _rms(x_ref[...], g_ref[...]).astype(BF16)
    cos = cos_ref[...]
    sin = sin_ref[...]
    row = lax.broadcasted_iota(jnp.int32, (LANES, LANES), 0) >= DIFF_QK_DIM
    col = lax.broadcasted_iota(jnp.int32, (LANES, LANES), 1) >= DIFF_QK_DIM
    group = jnp.where(row == col, 1.0, 0.0).astype(BF16)
    lane = lax.broadcasted_iota(jnp.int32, (1, LANES), 1)
    low_half = jnp.bitwise_and(lane, DIFF_QK_DIM // 2) == 0
    wide = 2 * LANES

    def head_pair(y, gain, scale):
        sq = y * y
        hi = sq.astype(BF16)
        lo = (sq - hi.astype(F32)).astype(BF16)
        ms = (_mm(hi, group) + _mm(lo, group)) * (1.0 / DIFF_QK_DIM)
        yn = y * lax.rsqrt(ms + EPS) * gain
        half = DIFF_QK_DIM // 2
        partner = jnp.where(low_half, pltpu.roll(yn, LANES - half, 1), pltpu.roll(yn, half, 1))
        return (yn * cos + partner * sin) * scale

    ones = jnp.ones((ATT_ONES_ROWS, x_ref.shape[0]), BF16)
    for cb in range(0, DIFF_TOTAL, wide):
        yq = _mm(xn, w_ref[:, cb:cb + wide])
        yk = _mm(xn, w_ref[:, DIFF_TOTAL + cb:DIFF_TOTAL + cb + wide])
        yv = _mm(xn, w_ref[:, 2 * DIFF_TOTAL + cb:2 * DIFF_TOTAL + cb + wide])
        for sub in range(0, wide, LANES):
            cols = slice(sub, sub + LANES)
            head = (cb + sub) // DIFF_V_DIM
            q = head_pair(yq[:, cols], qg_ref[...], DIFF_QK_DIM ** -0.5 * math.log2(math.e))
            qt_ref[0, cb + sub:cb + sub + LANES, :] = q.T.astype(BF16)
            k_ref[:, cb + sub:cb + sub + LANES] = head_pair(yk[:, cols], kg_ref[...], 1.0).astype(BF16)
            base = head * ATT_V_ROWS
            vt_ref[0, 0, base:base + DIFF_V_DIM, :] = yv[:, cols].T.astype(BF16)
            vt_ref[0, 0, base + DIFF_V_DIM:base + ATT_V_ROWS, :] = ones


def _c_in(x, g, w, qg, kg, cos, sin, seq):
    t, d = x.shape
    tm = _tile(seq, ATT_TK)
    npos = seq // tm
    b = t // seq
    tok = lambda i: (i, 0)
    pos = lambda i: (i % npos, 0)
    vrows = DIFF_HEADS * ATT_V_ROWS
    return pl.pallas_call(
        _c_in_kernel,
        grid=(t // tm,),
        in_specs=[
            pl.BlockSpec((tm, d), tok),
            _resident((1, d)),
            _resident(w.shape),
            _resident(qg.shape),
            _resident(kg.shape),
            pl.BlockSpec((tm, LANES), pos),
            pl.BlockSpec((tm, LANES), pos),
        ],
        out_specs=[
            pl.BlockSpec((1, DIFF_TOTAL, tm), lambda i: (i // npos, 0, i % npos)),
            pl.BlockSpec((tm, DIFF_TOTAL), tok),
            pl.BlockSpec((1, 1, vrows, tm), lambda i: (i // npos, i % npos, 0, 0)),
        ],
        out_shape=[
            jax.ShapeDtypeStruct((b, DIFF_TOTAL, seq), BF16),
            jax.ShapeDtypeStruct((t, DIFF_TOTAL), BF16),
            jax.ShapeDtypeStruct((b, npos, vrows, tm), BF16),
        ],
        compiler_params=_params("parallel"),
        name="c_in",
    )(x, g, w, qg, kg, cos, sin)


def _attn_kernel(lam_ref, sg_ref, qt_ref, k_ref, vt_ref, o_ref, acc_ref, s_ref, *, tq, tk, nk, lam_init):
    qt = qt_ref[0].astype(F32)
    row = lax.broadcasted_iota(jnp.int32, qt.shape, 0)
    qs = jnp.concatenate([jnp.where(row < DIFF_QK_DIM, qt, 0.0), jnp.where(row >= DIFF_QK_DIM, qt, 0.0)],
                         axis=1).astype(BF16)
    acc_ref[...] = jnp.zeros_like(acc_ref)

    def scores(j, slot):
        s = _mm(k_ref[0, pl.ds(pl.multiple_of(j * tk, tk), tk), :], qs)
        s_ref[slot] = s
        return jnp.max(s, axis=0, keepdims=True)

    def step(j, slot, carry, prefetch=True):
        m_prev, tile_max = carry
        next_max = scores(j + 1, 1 - slot) if prefetch else tile_max
        m_new = jnp.maximum(m_prev, tile_max)
        p = jnp.exp2(s_ref[slot] - m_new).astype(BF16)
        acc_ref[...] = jnp.exp2(m_prev - m_new) * acc_ref[...] + _mm(vt_ref[0, j], p)
        return m_new, next_max

    def pair(jj, carry):
        return step(2 * jj + 1, 1, step(2 * jj, 0, carry))

    carry = (jnp.full((1, 2 * tq), -jnp.inf, F32), scores(0, 0))
    carry = lax.fori_loop(0, nk // 2 - 1, pair, carry)
    step(nk - 1, 1, step(nk - 2, 0, carry), prefetch=False)
    lam = lam_ref[...]
    lam_full = (jnp.exp(jnp.sum(lam[0:1] * lam[1:2], axis=1, keepdims=True))
                - jnp.exp(jnp.sum(lam[2:3] * lam[3:4], axis=1, keepdims=True)) + lam_init)
    acc = acc_ref[...]
    out = acc[:DIFF_V_DIM] / acc[DIFF_V_DIM:DIFF_V_DIM + 1]
    o = out[:, :tq] - lam_full * out[:, tq:]
    o = o * lax.rsqrt(jnp.mean(o * o, axis=0, keepdims=True) + EPS)
    o_ref[0] = (o.T * sg_ref[...] * (1.0 - lam_init)).astype(BF16)


def _diff_attn(lam, sg, qt, k, vt, lam_init):
    b, s, d = k.shape
    nk, tk = vt.shape[1], vt.shape[3]
    assert nk % 2 == 0, "key tiles are consumed in pairs"
    tq = _tile(s, ATT_TQ)
    return pl.pallas_call(
        functools.partial(_attn_kernel, tq=tq, tk=tk, nk=nk, lam_init=lam_init),
        grid=(b, DIFF_HEADS, s // tq),
        in_specs=[
            _resident(lam.shape),
            _resident(sg.shape),
            pl.BlockSpec((1, DIFF_V_DIM, tq), lambda bi, h, qi: (bi, h, qi)),
            pl.BlockSpec((1, s, DIFF_V_DIM), lambda bi, h, qi: (bi, 0, h)),
            pl.BlockSpec((1, nk, ATT_V_ROWS, tk), lambda bi, h, qi: (bi, 0, h, 0)),
        ],
        out_specs=pl.BlockSpec((1, tq, DIFF_V_DIM), lambda bi, h, qi: (bi, qi, h)),
        out_shape=jax.ShapeDtypeStruct((b, s, d), BF16),
        scratch_shapes=[pltpu.VMEM((ATT_V_ROWS, 2 * tq), F32), pltpu.VMEM((2, tk, 2 * tq), F32)],
        compiler_params=_params("parallel", "parallel", "arbitrary"),
        name="diff_attn",
    )(lam, sg, qt, k, vt)


def _rope_tables(seq, head_dim):
    half = head_dim // 2
    inv = ROPE_THETA ** (-jnp.arange(half, dtype=F32) / half)
    ang = jnp.arange(seq, dtype=F32)[:, None] * inv[None, :]
    cos = jnp.cos(ang)
    sin = jnp.sin(ang)
    reps = LANES // head_dim
    return (jnp.tile(jnp.concatenate([cos, cos], axis=-1), (1, reps)),
            jnp.tile(jnp.concatenate([-sin, sin], axis=-1), (1, reps)))


def _ffn_weights(w_in, w_out):
    d, two_ff = w_in.shape
    ff = two_ff // 2
    tf = _tile(ff, FFN_TF)
    w = w_in.astype(BF16).reshape(d, 2, ff // tf, tf).transpose(1, 2, 0, 3)
    return w[0], w[1], w_out.astype(BF16).reshape(ff // tf, tf, d)


def _lambda_init(layer):
    return 0.8 - 0.6 * math.exp(-0.3 * layer)


def kernel(x_prompt, x_sample, norm_g, ffn_w_in, ffn_w_out, ab_w_in, ab_decay, ab_ret_norm_g, ab_conv_w, ab_conv_b,
           ab_conv_norm_g, ab_w_out, c_w_in, c_q_norm_g, c_k_norm_g, c_lambda, c_subln_g, c_w_out):
    depth = norm_g.shape[0]
    d = norm_g.shape[-1]
    ffn = [[_ffn_weights(ffn_w_in[l, j], ffn_w_out[l, j]) for j in range(2)] for l in range(depth)]
    ab_w_in_b = ab_w_in.astype(BF16)
    ab_w_out_b = ab_w_out.astype(BF16)
    c_w_in_b = c_w_in.astype(BF16)
    c_w_out_b = c_w_out.astype(BF16)

    def trunk(x3):
        b, s, _ = x3.shape
        x = x3.reshape(b * s, d)
        seq3 = lambda a: a.reshape(b, s, a.shape[-1])
        flat = lambda a: a.reshape(b * s, a.shape[-1])
        for layer in range(depth):
            g = norm_g[layer][:, None, :]
            x = _ffn(x, g[0], *ffn[layer][0])
            i = layer // 2
            if layer % 2 == 0:
                cos, sin = _rope_tables(s, RET_HEAD_DIM)
                q, k, v, gate, u = _ab_in(x, g[1], ab_w_in_b[i], cos, sin, s)
                ret = _retention(ab_decay[i], seq3(q), seq3(k), seq3(v), seq3(gate), ab_ret_norm_g[i][None])
                conv = _conv(seq3(u), ab_conv_w[i], ab_conv_b[i][None], ab_conv_norm_g[i][None])
                x = _out_proj(x, [flat(ret), flat(conv)], [ab_w_out_b[i, :RET_DIM], ab_w_out_b[i, RET_DIM:]])
            else:
                cos, sin = _rope_tables(s, DIFF_QK_DIM)
                reps = LANES // DIFF_QK_DIM
                qt, k, vt = _c_in(x, g[1], c_w_in_b[i], jnp.tile(c_q_norm_g[i], reps)[None],
                                  jnp.tile(c_k_norm_g[i], reps)[None], cos, sin, s)
                o = _diff_attn(c_lambda[i], c_subln_g[i][None], qt, seq3(k), vt, _lambda_init(layer))
                x = _out_proj(x, [flat(o)], [c_w_out_b[i]])
            x = _ffn(x, g[2], *ffn[layer][1])
        return x.reshape(b, s, d)

    return trunk(x_prompt), trunk(x_sample)
```

```python
import functools
import math

import jax
import jax.numpy as jnp
from jax import lax
from jax.experimental import pallas as pl
from jax.experimental.pallas import tpu as pltpu

F32 = jnp.float32
BF16 = jnp.bfloat16

EPS = 1e-6
ROPE_THETA = 10000.0
RET_HEADS = 4
RET_HEAD_DIM = 128
RET_DIM = RET_HEADS * RET_HEAD_DIM
RET_CHUNK = 128
CONV_DIM = 512
CONV_WIDTH = 31
CONV_PAD = CONV_WIDTH // 2
CONV_HALO = 16
DIFF_HEADS = 8
DIFF_QK_DIM = 64
DIFF_V_DIM = 128
DIFF_TOTAL = DIFF_HEADS * DIFF_V_DIM
LANES = 128
SUBLANES = 8

V7X_VMEM_BYTES = 64 * 1024 * 1024
VMEM_LIMIT = V7X_VMEM_BYTES - 8 * 1024 * 1024

FFN_TM = 512
FFN_TF = 256
PROJ_TM = 512
RET_TS = 512
CONV_TS = 512
CONV_RB = 64
ATT_TQ = 512
ATT_TK = 512
ATT_ONES_ROWS = 16
ATT_V_ROWS = DIFF_V_DIM + ATT_ONES_ROWS


def _tile(n, pref):
    t = min(n, pref)
    while n % t:
        t //= 2
    return t


def _params(*sem):
    return pltpu.CompilerParams(dimension_semantics=sem, vmem_limit_bytes=VMEM_LIMIT)


def _resident(shape):
    zeros = (0,) * len(shape)
    return pl.BlockSpec(shape, lambda *_: zeros, pipeline_mode=pl.Buffered(1))


def _rms(x, g):
    return x * lax.rsqrt(jnp.mean(x * x, axis=-1, keepdims=True) + EPS) * g


def _silu(x):
    return x * jax.nn.sigmoid(x)


def _mm(a, b):
    return jnp.dot(a, b, preferred_element_type=F32)


def _ffn_kernel(*refs, n_acts, tf):
    x_ref, act_refs, rest = refs[0], refs[1:1 + n_acts], refs[1 + n_acts:]
    wp_ref = rest[0] if n_acts else None
    g_ref, wi_ref, wo_ref, o_ref, xn_ref, acc_ref = rest[1:] if n_acts else rest
    x = x_ref[...]
    row = 0
    for a_ref in act_refs:
        x = x + _mm(a_ref[...], wp_ref[row:row + a_ref.shape[1], :])
        row += a_ref.shape[1]
    o_ref[...] = x
    xn_ref[...] = _rms(x, g_ref[...]).astype(BF16)
    acc_ref[...] = jnp.zeros_like(acc_ref)
    ff = wo_ref.shape[0]
    for lo in range(0, ff, tf):
        xn = xn_ref[...]
        h = _silu(_mm(xn, wi_ref[:, lo:lo + tf])) * _mm(xn, wi_ref[:, ff + lo:ff + lo + tf])
        acc_ref[...] += _mm(h.astype(BF16), wo_ref[lo:lo + tf, :])
    o_ref[...] += 0.5 * acc_ref[...]


def _ffn(x, acts, w_pre, g, w_in, w_out, layer, half):
    t, d = x.shape
    ff = w_out.shape[2]
    tm = _tile(t, FFN_TM)
    tok = lambda i: (i, 0)
    pick = lambda i: (layer, half, 0, 0)
    return pl.pallas_call(
        functools.partial(_ffn_kernel, n_acts=len(acts), tf=_tile(ff, FFN_TF)),
        grid=(t // tm,),
        in_specs=[pl.BlockSpec((tm, d), tok)]
        + [pl.BlockSpec((tm, a.shape[1]), tok) for a in acts]
        + ([_resident(w_pre.shape)] if acts else [])
        + [
            _resident((1, d)),
            pl.BlockSpec((None, None, d, 2 * ff), pick, pipeline_mode=pl.Buffered(1)),
            pl.BlockSpec((None, None, ff, d), pick, pipeline_mode=pl.Buffered(1)),
        ],
        out_specs=pl.BlockSpec((tm, d), tok),
        out_shape=jax.ShapeDtypeStruct((t, d), F32),
        scratch_shapes=[pltpu.VMEM((tm, d), BF16), pltpu.VMEM((tm, d), F32)],
        compiler_params=_params("parallel"),
        name="ffn",
    )(x, *acts, *([w_pre] if acts else []), g, w_in, w_out)


def _ab_in_kernel(x_ref, g_ref, w_ref, cos_ref, sin_ref, q_ref, k_ref, v_ref, gate_ref, u_ref):
    xn = _rms(x_ref[...], g_ref[...]).astype(BF16)
    cos = cos_ref[...]
    sin = sin_ref[...]

    def proj(seg):
        return _mm(xn, w_ref[:, seg * RET_DIM:(seg + 1) * RET_DIM])

    def rope(y):
        return y * cos + pltpu.roll(y, RET_HEAD_DIM // 2, 1) * sin

    q = proj(0)
    k = proj(1)
    for h in range(RET_HEADS):
        cols = slice(h * RET_HEAD_DIM, (h + 1) * RET_HEAD_DIM)
        q_ref[:, cols] = rope(q[:, cols])
        k_ref[:, cols] = rope(k[:, cols]) * RET_HEAD_DIM ** -0.5
    v_ref[...] = proj(2).astype(BF16)
    gate_ref[...] = proj(3)
    u_ref[...] = proj(4) * jax.nn.sigmoid(proj(5))


def _ab_in(x, g, w, cos, sin, seq):
    t, d = x.shape
    tm = _tile(seq, PROJ_TM)
    npos = seq // tm
    tok = lambda i: (i, 0)
    pos = lambda i: (i % npos, 0)
    wide = jax.ShapeDtypeStruct((t, RET_DIM), F32)
    return pl.pallas_call(
        _ab_in_kernel,
        grid=(t // tm,),
        in_specs=[
            pl.BlockSpec((tm, d), tok),
            _resident((1, d)),
            _resident(w.shape),
            pl.BlockSpec((tm, RET_HEAD_DIM), pos),
            pl.BlockSpec((tm, RET_HEAD_DIM), pos),
        ],
        out_specs=[pl.BlockSpec((tm, RET_DIM), tok)] * 5,
        out_shape=[wide, wide, jax.ShapeDtypeStruct((t, RET_DIM), BF16), wide, wide],
        compiler_params=_params("parallel"),
        name="ab_in",
    )(x, g, w, cos, sin)


def _ret_decays(decay_ref, direction, h):
    c = RET_CHUNK
    lg = -jnp.exp(jnp.full((1, 1), decay_ref[direction, h], F32))
    idx = lax.broadcasted_iota(jnp.int32, (c, 1), 0).astype(F32)
    if direction == 0:
        q_dec = jnp.exp(lg * (idx + 1.0))
        k_dec = jnp.exp(lg * (c - 1.0 - idx))
    else:
        q_dec = jnp.exp(lg * (c - idx))
        k_dec = jnp.exp(lg * idx)
    return lg, q_dec, k_dec, jnp.exp(lg * float(c))


def _ret_bwd_kernel(decay_ref, q_ref, k_ref, v_ref, o_ref, state_ref, *, ts):
    c = RET_CHUNK

    @pl.when(pl.program_id(1) == 0)
    def _():
        state_ref[...] = jnp.zeros_like(state_ref)

    dec = [_ret_decays(decay_ref, 1, h) for h in range(RET_HEADS)]
    for ci in reversed(range(ts // c)):
        rows = slice(ci * c, (ci + 1) * c)
        for h in range(RET_HEADS):
            cols = slice(h * RET_HEAD_DIM, (h + 1) * RET_HEAD_DIM)
            _, q_dec, k_dec, c_dec = dec[h]
            qc = q_ref[0, rows, cols]
            kc = k_ref[0, rows, cols]
            vc = v_ref[0, rows, cols]
            state = state_ref[h]
            o_ref[0, rows, cols] = _mm((qc * q_dec).astype(BF16), state.astype(BF16))
            state_ref[h] = state * c_dec + _mm((kc * k_dec).T.astype(BF16), vc)


def _ret_fwd_kernel(decay_ref, q_ref, k_ref, v_ref, ob_ref, gate_ref, gn_ref, o_ref, state_ref, *, ts):
    c = RET_CHUNK

    @pl.when(pl.program_id(1) == 0)
    def _():
        state_ref[...] = jnp.zeros_like(state_ref)

    diff = (lax.broadcasted_iota(jnp.int32, (c, c), 0) - lax.broadcasted_iota(jnp.int32, (c, c), 1)).astype(F32)
    dec = []
    for h in range(RET_HEADS):
        lgf, q_dec, k_dec, c_dec = _ret_decays(decay_ref, 0, h)
        lgb = _ret_decays(decay_ref, 1, h)[0]
        dmat = jnp.where(diff >= 0, jnp.exp(lgf * jnp.maximum(diff, 0.0)), jnp.exp(lgb * jnp.maximum(-diff, 0.0)))
        dec.append((dmat, q_dec, k_dec, c_dec))
    for ci in range(ts // c):
        rows = slice(ci * c, (ci + 1) * c)
        for h in range(RET_HEADS):
            cols = slice(h * RET_HEAD_DIM, (h + 1) * RET_HEAD_DIM)
            dmat, q_dec, k_dec, c_dec = dec[h]
            qc = q_ref[0, rows, cols]
            kc = k_ref[0, rows, cols]
            vc = v_ref[0, rows, cols]
            state = state_ref[h]
            scores = lax.dot_general(qc.astype(BF16), kc.astype(BF16), (((1,), (1,)), ((), ())),
                                     preferred_element_type=F32) * dmat
            o = _mm(scores.astype(BF16), vc) + _mm((qc * q_dec).astype(BF16), state.astype(BF16))
            o = o + ob_ref[0, rows, cols]
            state_ref[h] = state * c_dec + _mm((kc * k_dec).T.astype(BF16), vc)
            o_ref[0, rows, cols] = (_silu(gate_ref[0, rows, cols]) * _rms(o, gn_ref[:, cols])).astype(BF16)


def _retention(decay, q, k, v, gate, gn):
    b, s, d = q.shape
    ts = _tile(s, RET_TS)
    nt = s // ts
    state = pltpu.VMEM((RET_HEADS, RET_HEAD_DIM, RET_HEAD_DIM), F32)
    fwd = lambda bi, ti: (bi, ti, 0)
    rev = lambda bi, ti: (bi, nt - 1 - ti, 0)
    smem = pl.BlockSpec(memory_space=pltpu.SMEM)
    ob = pl.pallas_call(
        functools.partial(_ret_bwd_kernel, ts=ts),
        grid=(b, nt),
        in_specs=[smem] + [pl.BlockSpec((1, ts, d), rev)] * 3,
        out_specs=pl.BlockSpec((1, ts, d), rev),
        out_shape=jax.ShapeDtypeStruct((b, s, d), F32),
        scratch_shapes=[state],
        compiler_params=_params("parallel", "arbitrary"),
        name="ret_bwd",
    )(decay, q, k, v)
    return pl.pallas_call(
        functools.partial(_ret_fwd_kernel, ts=ts),
        grid=(b, nt),
        in_specs=[smem] + [pl.BlockSpec((1, ts, d), fwd)] * 5 + [_resident(gn.shape)],
        out_specs=pl.BlockSpec((1, ts, d), fwd),
        out_shape=jax.ShapeDtypeStruct((b, s, d), BF16),
        scratch_shapes=[state],
        compiler_params=_params("parallel", "arbitrary"),
        name="ret_fwd",
    )(decay, q, k, v, ob, gate, gn)


def _conv_kernel(u_ref, prev_ref, next_ref, w_ref, b_ref, g_ref, o_ref, buf_ref, *, ts, rb):
    i = pl.program_id(1)
    zero = jnp.zeros((CONV_HALO, CONV_DIM), F32)
    buf_ref[0, 0:CONV_HALO, :] = jnp.where(i > 0, prev_ref[0], zero)
    buf_ref[0, CONV_HALO:CONV_HALO + ts, :] = u_ref[0]
    buf_ref[0, CONV_HALO + ts:, :] = jnp.where(i < pl.num_programs(1) - 1, next_ref[0], zero)
    rows = ts + 2 * CONV_HALO - SUBLANES
    for r in range(1, SUBLANES):
        buf_ref[r, 0:rows, :] = buf_ref[0, r:r + rows, :]
    w = w_ref[...]
    first = CONV_HALO - CONV_PAD
    for r0 in range(0, ts, rb):
        acc = jnp.zeros((rb, CONV_DIM), F32)
        for t in range(CONV_WIDTH):
            shift, base = (r0 + first + t) % SUBLANES, (r0 + first + t) // SUBLANES * SUBLANES
            acc = acc + w[t:t + 1, :] * buf_ref[shift, base:base + rb, :]
        y = _rms(acc + b_ref[...], g_ref[...])
        o_ref[0, r0:r0 + rb, :] = _silu(y).astype(BF16)


def _conv(u, w, bias, g):
    b, s, d = u.shape
    ts = _tile(s, CONV_TS)
    nt = s // ts
    per = ts // CONV_HALO
    nh = s // CONV_HALO
    return pl.pallas_call(
        functools.partial(_conv_kernel, ts=ts, rb=_tile(ts, CONV_RB)),
        grid=(b, nt),
        in_specs=[
            pl.BlockSpec((1, ts, d), lambda bi, ti: (bi, ti, 0)),
            pl.BlockSpec((1, CONV_HALO, d), lambda bi, ti: (bi, jnp.maximum(ti * per - 1, 0), 0)),
            pl.BlockSpec((1, CONV_HALO, d), lambda bi, ti: (bi, jnp.minimum((ti + 1) * per, nh - 1), 0)),
            _resident(w.shape),
            _resident(bias.shape),
            _resident(g.shape),
        ],
        out_specs=pl.BlockSpec((1, ts, d), lambda bi, ti: (bi, ti, 0)),
        out_shape=jax.ShapeDtypeStruct((b, s, d), BF16),
        scratch_shapes=[pltpu.VMEM((SUBLANES, ts + 2 * CONV_HALO, d), F32)],
        compiler_params=_params("parallel", "parallel"),
        name="conv",
    )(u, u, u, w, bias, g)


def _c_in_kernel(x_ref, g_ref, w_ref, qg_ref, kg_ref, cos_ref, sin_ref, qt_ref, k_ref, vt_ref):
    xn = _rms(x_ref[...], g_ref[...]).astype(BF16)
    cos = cos_ref[...]
    sin = sin_ref[...]
    row = lax.broadcasted_iota(jnp.int32, (LANES, LANES), 0) >= DIFF_QK_DIM
    col = lax.broadcasted_iota(jnp.int32, (LANES, LANES), 1) >= DIFF_QK_DIM
    group = jnp.where(row == col, 1.0, 0.0).astype(BF16)
    lane = lax.broadcasted_iota(jnp.int32, (1, LANES), 1)
    low_half = jnp.bitwise_and(lane, DIFF_QK_DIM // 2) == 0
    wide = 2 * LANES

    def head_pair(y, gain, scale):
        sq = y * y
        hi = sq.astype(BF16)
        lo = (sq - hi.astype(F32)).astype(BF16)
        ms = (_mm(hi, group) + _mm(lo, group)) * (1.0 / DIFF_QK_DIM)
        yn = y * lax.rsqrt(ms + EPS) * gain
        half = DIFF_QK_DIM // 2
        partner = jnp.where(low_half, pltpu.roll(yn, LANES - half, 1), pltpu.roll(yn, half, 1))
        return (yn * cos + partner * sin) * scale

    ones = jnp.ones((ATT_ONES_ROWS, x_ref.shape[0]), BF16)
    for cb in range(0, DIFF_TOTAL, wide):
        yq = _mm(xn, w_ref[:, cb:cb + wide])
        yk = _mm(xn, w_ref[:, DIFF_TOTAL + cb:DIFF_TOTAL + cb + wide])
        yv = _mm(xn, w_ref[:, 2 * DIFF_TOTAL + cb:2 * DIFF_TOTAL + cb + wide])
        for sub in range(0, wide, LANES):
            cols = slice(sub, sub + LANES)
            head = (cb + sub) // DIFF_V_DIM
            q = head_pair(yq[:, cols], qg_ref[...], DIFF_QK_DIM ** -0.5 * math.log2(math.e))
            qt_ref[0, cb + sub:cb + sub + LANES, :] = q.T.astype(BF16)
            k_ref[:, cb + sub:cb + sub + LANES] = head_pair(yk[:, cols], kg_ref[...], 1.0).astype(BF16)
            base = head * ATT_V_ROWS
            vt_ref[0, 0, base:base + DIFF_V_DIM, :] = yv[:, cols].T.astype(BF16)
            vt_ref[0, 0, base + DIFF_V_DIM:base + ATT_V_ROWS, :] = ones


def _c_in(x, g, w, qg, kg, cos, sin, seq):
    t, d = x.shape
    tm = _tile(seq, ATT_TK)
    npos = seq // tm
    b = t // seq
    tok = lambda i: (i, 0)
    pos = lambda i: (i % npos, 0)
    vrows = DIFF_HEADS * ATT_V_ROWS
    return pl.pallas_call(
        _c_in_kernel,
        grid=(t // tm,),
        in_specs=[
            pl.BlockSpec((tm, d), tok),
            _resident((1, d)),
            _resident(w.shape),
            _resident(qg.shape),
            _resident(kg.shape),
            pl.BlockSpec((tm, LANES), pos),
            pl.BlockSpec((tm, LANES), pos),
        ],
        out_specs=[
            pl.BlockSpec((1, DIFF_TOTAL, tm), lambda i: (i // npos, 0, i % npos)),
            pl.BlockSpec((tm, DIFF_TOTAL), tok),
            pl.BlockSpec((1, 1, vrows, tm), lambda i: (i // npos, i % npos, 0, 0)),
        ],
        out_shape=[
            jax.ShapeDtypeStruct((b, DIFF_TOTAL, seq), BF16),
            jax.ShapeDtypeStruct((t, DIFF_TOTAL), BF16),
            jax.ShapeDtypeStruct((b, npos, vrows, tm), BF16),
        ],
        compiler_params=_params("parallel"),
        name="c_in",
    )(x, g, w, qg, kg, cos, sin)


def _attn_kernel(lam_ref, sg_ref, qt_ref, k_ref, vt_ref, o_ref, acc_ref, s_ref, *, tq, tk, nk, lam_init):
    qt = qt_ref[0].astype(F32)
    row = lax.broadcasted_iota(jnp.int32, qt.shape, 0)
    qs = jnp.concatenate([jnp.where(row < DIFF_QK_DIM, qt, 0.0), jnp.where(row >= DIFF_QK_DIM, qt, 0.0)],
                         axis=1).astype(BF16)
    acc_ref[...] = jnp.zeros_like(acc_ref)

    def scores(j, slot):
        s = _mm(k_ref[0, pl.ds(pl.multiple_of(j * tk, tk), tk), :], qs)
        s_ref[slot] = s
        return jnp.max(s, axis=0, keepdims=True)

    def step(j, slot, carry, prefetch=True):
        m_prev, tile_max = carry
        next_max = scores(j + 1, 1 - slot) if prefetch else tile_max
        m_new = jnp.maximum(m_prev, tile_max)
        p = jnp.exp2(s_ref[slot] - m_new).astype(BF16)
        acc_ref[...] = jnp.exp2(m_prev - m_new) * acc_ref[...] + _mm(vt_ref[0, j], p)
        return m_new, next_max

    def pair(jj, carry):
        return step(2 * jj + 1, 1, step(2 * jj, 0, carry))

    carry = (jnp.full((1, 2 * tq), -jnp.inf, F32), scores(0, 0))
    carry = lax.fori_loop(0, nk // 2 - 1, pair, carry)
    step(nk - 1, 1, step(nk - 2, 0, carry), prefetch=False)
    lam = lam_ref[...]
    lam_full = (jnp.exp(jnp.sum(lam[0:1] * lam[1:2], axis=1, keepdims=True))
                - jnp.exp(jnp.sum(lam[2:3] * lam[3:4], axis=1, keepdims=True)) + lam_init)
    acc = acc_ref[...]
    out = acc[:DIFF_V_DIM] / acc[DIFF_V_DIM:DIFF_V_DIM + 1]
    o = out[:, :tq] - lam_full * out[:, tq:]
    o = o * lax.rsqrt(jnp.mean(o * o, axis=0, keepdims=True) + EPS)
    o_ref[0] = (o.T * sg_ref[...] * (1.0 - lam_init)).astype(BF16)


def _diff_attn(lam, sg, qt, k, vt, lam_init):
    b, s, d = k.shape
    nk, tk = vt.shape[1], vt.shape[3]
    assert nk % 2 == 0, "key tiles are consumed in pairs"
    tq = _tile(s, ATT_TQ)
    return pl.pallas_call(
        functools.partial(_attn_kernel, tq=tq, tk=tk, nk=nk, lam_init=lam_init),
        grid=(b, DIFF_HEADS, s // tq),
        in_specs=[
            _resident(lam.shape),
            _resident(sg.shape),
            pl.BlockSpec((1, DIFF_V_DIM, tq), lambda bi, h, qi: (bi, h, qi)),
            pl.BlockSpec((1, s, DIFF_V_DIM), lambda bi, h, qi: (bi, 0, h)),
            pl.BlockSpec((1, nk, ATT_V_ROWS, tk), lambda bi, h, qi: (bi, 0, h, 0)),
        ],
        out_specs=pl.BlockSpec((1, tq, DIFF_V_DIM), lambda bi, h, qi: (bi, qi, h)),
        out_shape=jax.ShapeDtypeStruct((b, s, d), BF16),
        scratch_shapes=[pltpu.VMEM((ATT_V_ROWS, 2 * tq), F32), pltpu.VMEM((2, tk, 2 * tq), F32)],
        compiler_params=_params("parallel", "parallel", "arbitrary"),
        name="diff_attn",
    )(lam, sg, qt, k, vt)


def _rope_tables(seq, head_dim):
    half = head_dim // 2
    inv = ROPE_THETA ** (-jnp.arange(half, dtype=F32) / half)
    ang = jnp.arange(seq, dtype=F32)[:, None] * inv[None, :]
    cos = jnp.cos(ang)
    sin = jnp.sin(ang)
    reps = LANES // head_dim
    return (jnp.tile(jnp.concatenate([cos, cos], axis=-1), (1, reps)),
            jnp.tile(jnp.concatenate([-sin, sin], axis=-1), (1, reps)))


def _lambda_init(layer):
    return 0.8 - 0.6 * math.exp(-0.3 * layer)


def kernel(x_prompt, x_sample, norm_g, ffn_w_in, ffn_w_out, ab_w_in, ab_decay, ab_ret_norm_g, ab_conv_w, ab_conv_b,
           ab_conv_norm_g, ab_w_out, c_w_in, c_q_norm_g, c_k_norm_g, c_lambda, c_subln_g, c_w_out):
    depth = norm_g.shape[0]
    d = norm_g.shape[-1]
    ffn_w_in_b = ffn_w_in.astype(BF16)
    ffn_w_out_b = ffn_w_out.astype(BF16)
    ab_w_in_b = ab_w_in.astype(BF16)
    ab_w_out_b = ab_w_out.astype(BF16)
    c_w_in_b = c_w_in.astype(BF16)
    c_w_out_b = c_w_out.astype(BF16)

    def trunk(x3):
        b, s, _ = x3.shape
        x = x3.reshape(b * s, d)
        seq3 = lambda a: a.reshape(b, s, a.shape[-1])
        flat = lambda a: a.reshape(b * s, a.shape[-1])
        for layer in range(depth):
            g = norm_g[layer][:, None, :]
            x = _ffn(x, [], None, g[0], ffn_w_in_b, ffn_w_out_b, layer, 0)
            i = layer // 2
            if layer % 2 == 0:
                cos, sin = _rope_tables(s, RET_HEAD_DIM)
                q, k, v, gate, u = _ab_in(x, g[1], ab_w_in_b[i], cos, sin, s)
                ret = _retention(ab_decay[i], seq3(q), seq3(k), seq3(v), seq3(gate), ab_ret_norm_g[i][None])
                conv = _conv(seq3(u), ab_conv_w[i], ab_conv_b[i][None], ab_conv_norm_g[i][None])
                acts, w_pre = [flat(ret), flat(conv)], ab_w_out_b[i]
            else:
                cos, sin = _rope_tables(s, DIFF_QK_DIM)
                reps = LANES // DIFF_QK_DIM
                qt, k, vt = _c_in(x, g[1], c_w_in_b[i], jnp.tile(c_q_norm_g[i], reps)[None],
                                  jnp.tile(c_k_norm_g[i], reps)[None], cos, sin, s)
                o = _diff_attn(c_lambda[i], c_subln_g[i][None], qt, seq3(k), vt, _lambda_init(layer))
                acts, w_pre = [flat(o)], c_w_out_b[i]
            x = _ffn(x, acts, w_pre, g[2], ffn_w_in_b, ffn_w_out_b, layer, 1)
        return x.reshape(b, s, d)

    return trunk(x_prompt), trunk(x_sample)
```

```python
import functools
import math

import jax
import jax.numpy as jnp
from jax import lax
from jax.experimental import pallas as pl
from jax.experimental.pallas import tpu as pltpu

F32 = jnp.float32
BF16 = jnp.bfloat16

EPS = 1e-6
ROPE_THETA = 10000.0
RET_HEADS = 4
RET_HEAD_DIM = 128
RET_DIM = RET_HEADS * RET_HEAD_DIM
RET_CHUNK = 128
CONV_DIM = 512
CONV_WIDTH = 31
CONV_PAD = CONV_WIDTH // 2
CONV_HALO = 16
DIFF_HEADS = 8
DIFF_QK_DIM = 64
DIFF_V_DIM = 128
DIFF_TOTAL = DIFF_HEADS * DIFF_V_DIM
LANES = 128
SUBLANES = 8

V7X_VMEM_BYTES = 64 * 1024 * 1024
VMEM_LIMIT = V7X_VMEM_BYTES - 8 * 1024 * 1024

FFN_TM = 512
FFN_TF = 256
PROJ_TM = 512
RET_TS = 512
CONV_TS = 512
CONV_RB = 64
ATT_TQ = 512
ATT_TK = 512
ATT_ONES_ROWS = 16
ATT_V_ROWS = DIFF_V_DIM + ATT_ONES_ROWS
ATT_SAFE_LOG2 = 50.0
Q_SCALE_LOG2 = DIFF_QK_DIM ** -0.5 * math.log2(math.e)


def _tile(n, pref):
    t = min(n, pref)
    while n % t:
        t //= 2
    return t


def _params(*sem):
    return pltpu.CompilerParams(dimension_semantics=sem, vmem_limit_bytes=VMEM_LIMIT)


def _resident(shape):
    zeros = (0,) * len(shape)
    return pl.BlockSpec(shape, lambda *_: zeros, pipeline_mode=pl.Buffered(1))


def _rms(x, g):
    return x * lax.rsqrt(jnp.mean(x * x, axis=-1, keepdims=True) + EPS) * g


def _silu(x):
    return x * jax.nn.sigmoid(x)


def _mm(a, b):
    return jnp.dot(a, b, preferred_element_type=F32)


def _ffn_kernel(*refs, n_acts, tf):
    x_ref, act_refs, rest = refs[0], refs[1:1 + n_acts], refs[1 + n_acts:]
    wp_ref = rest[0] if n_acts else None
    g_ref, wi_ref, wo_ref, o_ref, xn_ref, acc_ref = rest[1:] if n_acts else rest
    x = x_ref[...]
    row = 0
    for a_ref in act_refs:
        x = x + _mm(a_ref[...], wp_ref[row:row + a_ref.shape[1], :])
        row += a_ref.shape[1]
    o_ref[...] = x
    xn_ref[...] = _rms(x, g_ref[...]).astype(BF16)
    acc_ref[...] = jnp.zeros_like(acc_ref)
    ff = wo_ref.shape[0]
    for lo in range(0, ff, tf):
        xn = xn_ref[...]
        h = _silu(_mm(xn, wi_ref[:, lo:lo + tf])) * _mm(xn, wi_ref[:, ff + lo:ff + lo + tf])
        acc_ref[...] += _mm(h.astype(BF16), wo_ref[lo:lo + tf, :])
    o_ref[...] += 0.5 * acc_ref[...]


def _ffn(x, acts, w_pre, g, w_in, w_out, layer, half):
    t, d = x.shape
    ff = w_out.shape[2]
    tm = _tile(t, FFN_TM)
    tok = lambda i: (i, 0)
    pick = lambda i: (layer, half, 0, 0)
    return pl.pallas_call(
        functools.partial(_ffn_kernel, n_acts=len(acts), tf=_tile(ff, FFN_TF)),
        grid=(t // tm,),
        in_specs=[pl.BlockSpec((tm, d), tok)]
        + [pl.BlockSpec((tm, a.shape[1]), tok) for a in acts]
        + ([_resident(w_pre.shape)] if acts else [])
        + [
            _resident((1, d)),
            pl.BlockSpec((None, None, d, 2 * ff), pick, pipeline_mode=pl.Buffered(1)),
            pl.BlockSpec((None, None, ff, d), pick, pipeline_mode=pl.Buffered(1)),
        ],
        out_specs=pl.BlockSpec((tm, d), tok),
        out_shape=jax.ShapeDtypeStruct((t, d), F32),
        scratch_shapes=[pltpu.VMEM((tm, d), BF16), pltpu.VMEM((tm, d), F32)],
        compiler_params=_params("parallel"),
        name="ffn",
    )(x, *acts, *([w_pre] if acts else []), g, w_in, w_out)


def _ab_in_kernel(x_ref, g_ref, w_ref, cos_ref, sin_ref, q_ref, k_ref, v_ref, gate_ref, u_ref):
    xn = _rms(x_ref[...], g_ref[...]).astype(BF16)
    cos = cos_ref[...]
    sin = sin_ref[...]

    def proj(seg):
        return _mm(xn, w_ref[:, seg * RET_DIM:(seg + 1) * RET_DIM])

    def rope(y):
        return y * cos + pltpu.roll(y, RET_HEAD_DIM // 2, 1) * sin

    q = proj(0)
    k = proj(1)
    for h in range(RET_HEADS):
        cols = slice(h * RET_HEAD_DIM, (h + 1) * RET_HEAD_DIM)
        q_ref[:, cols] = rope(q[:, cols])
        k_ref[:, cols] = rope(k[:, cols]) * RET_HEAD_DIM ** -0.5
    v_ref[...] = proj(2).astype(BF16)
    gate_ref[...] = proj(3)
    u_ref[...] = proj(4) * jax.nn.sigmoid(proj(5))


def _ab_in(x, g, w, cos, sin, seq):
    t, d = x.shape
    tm = _tile(seq, PROJ_TM)
    npos = seq // tm
    tok = lambda i: (i, 0)
    pos = lambda i: (i % npos, 0)
    wide = jax.ShapeDtypeStruct((t, RET_DIM), F32)
    return pl.pallas_call(
        _ab_in_kernel,
        grid=(t // tm,),
        in_specs=[
            pl.BlockSpec((tm, d), tok),
            _resident((1, d)),
            _resident(w.shape),
            pl.BlockSpec((tm, RET_HEAD_DIM), pos),
            pl.BlockSpec((tm, RET_HEAD_DIM), pos),
        ],
        out_specs=[pl.BlockSpec((tm, RET_DIM), tok)] * 5,
        out_shape=[wide, wide, jax.ShapeDtypeStruct((t, RET_DIM), BF16), wide, wide],
        compiler_params=_params("parallel"),
        name="ab_in",
    )(x, g, w, cos, sin)


def _ret_decays(decay_ref, direction, h):
    c = RET_CHUNK
    lg = -jnp.exp(jnp.full((1, 1), decay_ref[direction, h], F32))
    idx = lax.broadcasted_iota(jnp.int32, (c, 1), 0).astype(F32)
    if direction == 0:
        q_dec = jnp.exp(lg * (idx + 1.0))
        k_dec = jnp.exp(lg * (c - 1.0 - idx))
    else:
        q_dec = jnp.exp(lg * (c - idx))
        k_dec = jnp.exp(lg * idx)
    return lg, q_dec, k_dec, jnp.exp(lg * float(c))


def _ret_bwd_kernel(decay_ref, q_ref, k_ref, v_ref, o_ref, state_ref, *, ts):
    c = RET_CHUNK

    @pl.when(pl.program_id(1) == 0)
    def _():
        state_ref[...] = jnp.zeros_like(state_ref)

    dec = [_ret_decays(decay_ref, 1, h) for h in range(RET_HEADS)]
    for ci in reversed(range(ts // c)):
        rows = slice(ci * c, (ci + 1) * c)
        for h in range(RET_HEADS):
            cols = slice(h * RET_HEAD_DIM, (h + 1) * RET_HEAD_DIM)
            _, q_dec, k_dec, c_dec = dec[h]
            qc = q_ref[0, rows, cols]
            kc = k_ref[0, rows, cols]
            vc = v_ref[0, rows, cols]
            state = state_ref[h]
            o_ref[0, rows, cols] = _mm((qc * q_dec).astype(BF16), state.astype(BF16))
            state_ref[h] = state * c_dec + _mm((kc * k_dec).T.astype(BF16), vc)


def _ret_fwd_kernel(decay_ref, q_ref, k_ref, v_ref, ob_ref, gate_ref, gn_ref, o_ref, state_ref, *, ts):
    c = RET_CHUNK

    @pl.when(pl.program_id(1) == 0)
    def _():
        state_ref[...] = jnp.zeros_like(state_ref)

    diff = (lax.broadcasted_iota(jnp.int32, (c, c), 0) - lax.broadcasted_iota(jnp.int32, (c, c), 1)).astype(F32)
    dec = []
    for h in range(RET_HEADS):
        lgf, q_dec, k_dec, c_dec = _ret_decays(decay_ref, 0, h)
        lgb = _ret_decays(decay_ref, 1, h)[0]
        dmat = jnp.where(diff >= 0, jnp.exp(lgf * jnp.maximum(diff, 0.0)), jnp.exp(lgb * jnp.maximum(-diff, 0.0)))
        dec.append((dmat, q_dec, k_dec, c_dec))
    for ci in range(ts // c):
        rows = slice(ci * c, (ci + 1) * c)
        for h in range(RET_HEADS):
            cols = slice(h * RET_HEAD_DIM, (h + 1) * RET_HEAD_DIM)
            dmat, q_dec, k_dec, c_dec = dec[h]
            qc = q_ref[0, rows, cols]
            kc = k_ref[0, rows, cols]
            vc = v_ref[0, rows, cols]
            state = state_ref[h]
            scores = lax.dot_general(qc.astype(BF16), kc.astype(BF16), (((1,), (1,)), ((), ())),
                                     preferred_element_type=F32) * dmat
            o = _mm(scores.astype(BF16), vc) + _mm((qc * q_dec).astype(BF16), state.astype(BF16))
            o = o + ob_ref[0, rows, cols]
            state_ref[h] = state * c_dec + _mm((kc * k_dec).T.astype(BF16), vc)
            o_ref[0, rows, cols] = (_silu(gate_ref[0, rows, cols]) * _rms(o, gn_ref[:, cols])).astype(BF16)


def _retention(decay, q, k, v, gate, gn):
    b, s, d = q.shape
    ts = _tile(s, RET_TS)
    nt = s // ts
    state = pltpu.VMEM((RET_HEADS, RET_HEAD_DIM, RET_HEAD_DIM), F32)
    fwd = lambda bi, ti: (bi, ti, 0)
    rev = lambda bi, ti: (bi, nt - 1 - ti, 0)
    smem = pl.BlockSpec(memory_space=pltpu.SMEM)
    ob = pl.pallas_call(
        functools.partial(_ret_bwd_kernel, ts=ts),
        grid=(b, nt),
        in_specs=[smem] + [pl.BlockSpec((1, ts, d), rev)] * 3,
        out_specs=pl.BlockSpec((1, ts, d), rev),
        out_shape=jax.ShapeDtypeStruct((b, s, d), F32),
        scratch_shapes=[state],
        compiler_params=_params("parallel", "arbitrary"),
        name="ret_bwd",
    )(decay, q, k, v)
    return pl.pallas_call(
        functools.partial(_ret_fwd_kernel, ts=ts),
        grid=(b, nt),
        in_specs=[smem] + [pl.BlockSpec((1, ts, d), fwd)] * 5 + [_resident(gn.shape)],
        out_specs=pl.BlockSpec((1, ts, d), fwd),
        out_shape=jax.ShapeDtypeStruct((b, s, d), BF16),
        scratch_shapes=[state],
        compiler_params=_params("parallel", "arbitrary"),
        name="ret_fwd",
    )(decay, q, k, v, ob, gate, gn)


def _conv_kernel(u_ref, prev_ref, next_ref, w_ref, b_ref, g_ref, o_ref, buf_ref, *, ts, rb):
    i = pl.program_id(1)
    zero = jnp.zeros((CONV_HALO, CONV_DIM), F32)
    buf_ref[0, 0:CONV_HALO, :] = jnp.where(i > 0, prev_ref[0], zero)
    buf_ref[0, CONV_HALO:CONV_HALO + ts, :] = u_ref[0]
    buf_ref[0, CONV_HALO + ts:, :] = jnp.where(i < pl.num_programs(1) - 1, next_ref[0], zero)
    rows = ts + 2 * CONV_HALO - SUBLANES
    for r in range(1, SUBLANES):
        buf_ref[r, 0:rows, :] = buf_ref[0, r:r + rows, :]
    w = w_ref[...]
    first = CONV_HALO - CONV_PAD
    for r0 in range(0, ts, rb):
        acc = jnp.zeros((rb, CONV_DIM), F32)
        for t in range(CONV_WIDTH):
            shift, base = (r0 + first + t) % SUBLANES, (r0 + first + t) // SUBLANES * SUBLANES
            acc = acc + w[t:t + 1, :] * buf_ref[shift, base:base + rb, :]
        y = _rms(acc + b_ref[...], g_ref[...])
        o_ref[0, r0:r0 + rb, :] = _silu(y).astype(BF16)


def _conv(u, w, bias, g):
    b, s, d = u.shape
    ts = _tile(s, CONV_TS)
    nt = s // ts
    per = ts // CONV_HALO
    nh = s // CONV_HALO
    return pl.pallas_call(
        functools.partial(_conv_kernel, ts=ts, rb=_tile(ts, CONV_RB)),
        grid=(b, nt),
        in_specs=[
            pl.BlockSpec((1, ts, d), lambda bi, ti: (bi, ti, 0)),
            pl.BlockSpec((1, CONV_HALO, d), lambda bi, ti: (bi, jnp.maximum(ti * per - 1, 0), 0)),
            pl.BlockSpec((1, CONV_HALO, d), lambda bi, ti: (bi, jnp.minimum((ti + 1) * per, nh - 1), 0)),
            _resident(w.shape),
            _resident(bias.shape),
            _resident(g.shape),
        ],
        out_specs=pl.BlockSpec((1, ts, d), lambda bi, ti: (bi, ti, 0)),
        out_shape=jax.ShapeDtypeStruct((b, s, d), BF16),
        scratch_shapes=[pltpu.VMEM((SUBLANES, ts + 2 * CONV_HALO, d), F32)],
        compiler_params=_params("parallel", "parallel"),
        name="conv",
    )(u, u, u, w, bias, g)


def _c_in_kernel(x_ref, g_ref, w_ref, qg_ref, kg_ref, cos_ref, sin_ref, qt_ref, k_ref, vt_ref):
    xn = _rms(x_ref[...], g_ref[...]).astype(BF16)
    cos = cos_ref[...]
    sin = sin_ref[...]
    row = lax.broadcasted_iota(jnp.int32, (LANES, LANES), 0) >= DIFF_QK_DIM
    col = lax.broadcasted_iota(jnp.int32, (LANES, LANES), 1) >= DIFF_QK_DIM
    group = jnp.where(row == col, 1.0, 0.0).astype(BF16)
    lane = lax.broadcasted_iota(jnp.int32, (1, LANES), 1)
    low_half = jnp.bitwise_and(lane, DIFF_QK_DIM // 2) == 0
    wide = 2 * LANES

    def head_pair(y, gain, scale):
        sq = y * y
        hi = sq.astype(BF16)
        lo = (sq - hi.astype(F32)).astype(BF16)
        ms = (_mm(hi, group) + _mm(lo, group)) * (1.0 / DIFF_QK_DIM)
        yn = y * lax.rsqrt(ms + EPS) * gain
        half = DIFF_QK_DIM // 2
        partner = jnp.where(low_half, pltpu.roll(yn, LANES - half, 1), pltpu.roll(yn, half, 1))
        return (yn * cos + partner * sin) * scale

    ones = jnp.ones((ATT_ONES_ROWS, x_ref.shape[0]), BF16)
    for cb in range(0, DIFF_TOTAL, wide):
        yq = _mm(xn, w_ref[:, cb:cb + wide])
        yk = _mm(xn, w_ref[:, DIFF_TOTAL + cb:DIFF_TOTAL + cb + wide])
        yv = _mm(xn, w_ref[:, 2 * DIFF_TOTAL + cb:2 * DIFF_TOTAL + cb + wide])
        for sub in range(0, wide, LANES):
            cols = slice(sub, sub + LANES)
            head = (cb + sub) // DIFF_V_DIM
            q = head_pair(yq[:, cols], qg_ref[...], Q_SCALE_LOG2)
            qt_ref[0, cb + sub:cb + sub + LANES, :] = q.T.astype(BF16)
            k_ref[:, cb + sub:cb + sub + LANES] = head_pair(yk[:, cols], kg_ref[...], 1.0).astype(BF16)
            base = head * ATT_V_ROWS
            vt_ref[0, 0, base:base + DIFF_V_DIM, :] = yv[:, cols].T.astype(BF16)
            vt_ref[0, 0, base + DIFF_V_DIM:base + ATT_V_ROWS, :] = ones


def _c_in(x, g, w, qg, kg, cos, sin, seq):
    t, d = x.shape
    tm = _tile(seq, ATT_TK)
    npos = seq // tm
    b = t // seq
    tok = lambda i: (i, 0)
    pos = lambda i: (i % npos, 0)
    vrows = DIFF_HEADS * ATT_V_ROWS
    return pl.pallas_call(
        _c_in_kernel,
        grid=(t // tm,),
        in_specs=[
            pl.BlockSpec((tm, d), tok),
            _resident((1, d)),
            _resident(w.shape),
            _resident(qg.shape),
            _resident(kg.shape),
            pl.BlockSpec((tm, LANES), pos),
            pl.BlockSpec((tm, LANES), pos),
        ],
        out_specs=[
            pl.BlockSpec((1, DIFF_TOTAL, tm), lambda i: (i // npos, 0, i % npos)),
            pl.BlockSpec((tm, DIFF_TOTAL), tok),
            pl.BlockSpec((1, 1, vrows, tm), lambda i: (i // npos, i % npos, 0, 0)),
        ],
        out_shape=[
            jax.ShapeDtypeStruct((b, DIFF_TOTAL, seq), BF16),
            jax.ShapeDtypeStruct((t, DIFF_TOTAL), BF16),
            jax.ShapeDtypeStruct((b, npos, vrows, tm), BF16),
        ],
        compiler_params=_params("parallel"),
        name="c_in",
    )(x, g, w, qg, kg, cos, sin)


def _attn_queries(qt_ref):
    qt = qt_ref[0].astype(F32)
    row = lax.broadcasted_iota(jnp.int32, qt.shape, 0)
    return jnp.concatenate([jnp.where(row < DIFF_QK_DIM, qt, 0.0), jnp.where(row >= DIFF_QK_DIM, qt, 0.0)],
                           axis=1).astype(BF16)


def _attn_finish(lam_ref, sg_ref, acc_ref, o_ref, *, tq, lam_init):
    lam = lam_ref[...]
    lam_full = (jnp.exp(jnp.sum(lam[0:1] * lam[1:2], axis=1, keepdims=True))
                - jnp.exp(jnp.sum(lam[2:3] * lam[3:4], axis=1, keepdims=True)) + lam_init)
    acc = acc_ref[...]
    out = acc[:DIFF_V_DIM] / acc[DIFF_V_DIM:DIFF_V_DIM + 1]
    o = out[:, :tq] - lam_full * out[:, tq:]
    o = o * lax.rsqrt(jnp.mean(o * o, axis=0, keepdims=True) + EPS)
    o_ref[0] = (o.T * sg_ref[...] * (1.0 - lam_init)).astype(BF16)


def _attn_bounded_kernel(lam_ref, sg_ref, qt_ref, k_ref, vt_ref, o_ref, acc_ref, *, tq, tk, nk, lam_init):
    qs = _attn_queries(qt_ref)
    acc_ref[...] = jnp.zeros_like(acc_ref)

    def body(j, carry):
        kj = k_ref[0, pl.ds(pl.multiple_of(j * tk, tk), tk), :]
        acc_ref[...] += _mm(vt_ref[0, j], jnp.exp2(_mm(kj, qs)).astype(BF16))
        return carry

    lax.fori_loop(0, nk, body, 0, unroll=8)
    _attn_finish(lam_ref, sg_ref, acc_ref, o_ref, tq=tq, lam_init=lam_init)


def _attn_online_kernel(lam_ref, sg_ref, qt_ref, k_ref, vt_ref, o_ref, acc_ref, s_ref, *, tq, tk, nk, lam_init):
    qs = _attn_queries(qt_ref)
    acc_ref[...] = jnp.zeros_like(acc_ref)

    def scores(j, slot):
        s = _mm(k_ref[0, pl.ds(pl.multiple_of(j * tk, tk), tk), :], qs)
        s_ref[slot] = s
        return jnp.max(s, axis=0, keepdims=True)

    def step(j, slot, carry, prefetch=True):
        m_prev, tile_max = carry
        next_max = scores(j + 1, 1 - slot) if prefetch else tile_max
        m_new = jnp.maximum(m_prev, tile_max)
        p = jnp.exp2(s_ref[slot] - m_new).astype(BF16)
        acc_ref[...] = jnp.exp2(m_prev - m_new) * acc_ref[...] + _mm(vt_ref[0, j], p)
        return m_new, next_max

    def pair(jj, carry):
        return step(2 * jj + 1, 1, step(2 * jj, 0, carry))

    carry = (jnp.full((1, 2 * tq), -jnp.inf, F32), scores(0, 0))
    carry = lax.fori_loop(0, nk // 2 - 1, pair, carry)
    step(nk - 1, 1, step(nk - 2, 0, carry), prefetch=False)
    _attn_finish(lam_ref, sg_ref, acc_ref, o_ref, tq=tq, lam_init=lam_init)


def _diff_attn(lam, sg, qt, k, vt, lam_init, score_bound):
    b, s, d = k.shape
    nk, tk = vt.shape[1], vt.shape[3]
    assert nk % 2 == 0, "key tiles are consumed in pairs"
    tq = _tile(s, ATT_TQ)

    def call(body, name, extra_scratch):
        return pl.pallas_call(
            functools.partial(body, tq=tq, tk=tk, nk=nk, lam_init=lam_init),
            grid=(b, DIFF_HEADS, s // tq),
            in_specs=[
                _resident(lam.shape),
                _resident(sg.shape),
                pl.BlockSpec((1, DIFF_V_DIM, tq), lambda bi, h, qi: (bi, h, qi)),
                pl.BlockSpec((1, s, DIFF_V_DIM), lambda bi, h, qi: (bi, 0, h)),
                pl.BlockSpec((1, nk, ATT_V_ROWS, tk), lambda bi, h, qi: (bi, 0, h, 0)),
            ],
            out_specs=pl.BlockSpec((1, tq, DIFF_V_DIM), lambda bi, h, qi: (bi, qi, h)),
            out_shape=jax.ShapeDtypeStruct((b, s, d), BF16),
            scratch_shapes=[pltpu.VMEM((ATT_V_ROWS, 2 * tq), F32)] + extra_scratch,
            compiler_params=_params("parallel", "parallel", "arbitrary"),
            name=name,
        )

    bounded = call(_attn_bounded_kernel, "diff_attn_bounded", [])
    online = call(_attn_online_kernel, "diff_attn_online", [pltpu.VMEM((2, tk, 2 * tq), F32)])
    return lax.cond(score_bound <= ATT_SAFE_LOG2, bounded, online, lam, sg, qt, k, vt)


def _rope_tables(seq, head_dim):
    half = head_dim // 2
    inv = ROPE_THETA ** (-jnp.arange(half, dtype=F32) / half)
    ang = jnp.arange(seq, dtype=F32)[:, None] * inv[None, :]
    cos = jnp.cos(ang)
    sin = jnp.sin(ang)
    reps = LANES // head_dim
    return (jnp.tile(jnp.concatenate([cos, cos], axis=-1), (1, reps)),
            jnp.tile(jnp.concatenate([-sin, sin], axis=-1), (1, reps)))


def _lambda_init(layer):
    return 0.8 - 0.6 * math.exp(-0.3 * layer)


def kernel(x_prompt, x_sample, norm_g, ffn_w_in, ffn_w_out, ab_w_in, ab_decay, ab_ret_norm_g, ab_conv_w, ab_conv_b,
           ab_conv_norm_g, ab_w_out, c_w_in, c_q_norm_g, c_k_norm_g, c_lambda, c_subln_g, c_w_out):
    depth = norm_g.shape[0]
    d = norm_g.shape[-1]
    ffn_w_in_b = ffn_w_in.astype(BF16)
    ffn_w_out_b = ffn_w_out.astype(BF16)
    ab_w_in_b = ab_w_in.astype(BF16)
    ab_w_out_b = ab_w_out.astype(BF16)
    c_w_in_b = c_w_in.astype(BF16)
    c_w_out_b = c_w_out.astype(BF16)

    def trunk(x3):
        b, s, _ = x3.shape
        x = x3.reshape(b * s, d)
        seq3 = lambda a: a.reshape(b, s, a.shape[-1])
        flat = lambda a: a.reshape(b * s, a.shape[-1])
        for layer in range(depth):
            g = norm_g[layer][:, None, :]
            x = _ffn(x, [], None, g[0], ffn_w_in_b, ffn_w_out_b, layer, 0)
            i = layer // 2
            if layer % 2 == 0:
                cos, sin = _rope_tables(s, RET_HEAD_DIM)
                q, k, v, gate, u = _ab_in(x, g[1], ab_w_in_b[i], cos, sin, s)
                ret = _retention(ab_decay[i], seq3(q), seq3(k), seq3(v), seq3(gate), ab_ret_norm_g[i][None])
                conv = _conv(seq3(u), ab_conv_w[i], ab_conv_b[i][None], ab_conv_norm_g[i][None])
                acts, w_pre = [flat(ret), flat(conv)], ab_w_out_b[i]
            else:
                cos, sin = _rope_tables(s, DIFF_QK_DIM)
                reps = LANES // DIFF_QK_DIM
                qt, k, vt = _c_in(x, g[1], c_w_in_b[i], jnp.tile(c_q_norm_g[i], reps)[None],
                                  jnp.tile(c_k_norm_g[i], reps)[None], cos, sin, s)
                bound = (DIFF_QK_DIM * Q_SCALE_LOG2 * (1 + 2.0 ** -7) * jnp.max(jnp.abs(c_q_norm_g[i]))
                         * jnp.max(jnp.abs(c_k_norm_g[i])))
                o = _diff_attn(c_lambda[i], c_subln_g[i][None], qt, seq3(k), vt, _lambda_init(layer), bound)
                acts, w_pre = [flat(o)], c_w_out_b[i]
            x = _ffn(x, acts, w_pre, g[2], ffn_w_in_b, ffn_w_out_b, layer, 1)
        return x.reshape(b, s, d)

    return trunk(x_prompt), trunk(x_sample)
```

```python
import functools
import math

import jax
import jax.numpy as jnp
from jax import lax
from jax.experimental import pallas as pl
from jax.experimental.pallas import tpu as pltpu

F32 = jnp.float32
BF16 = jnp.bfloat16

EPS = 1e-6
ROPE_THETA = 10000.0
RET_HEADS = 4
RET_HEAD_DIM = 128
RET_DIM = RET_HEADS * RET_HEAD_DIM
RET_CHUNK = 128
CONV_DIM = 512
CONV_WIDTH = 31
CONV_PAD = CONV_WIDTH // 2
CONV_HALO = 16
DIFF_HEADS = 8
DIFF_QK_DIM = 64
DIFF_V_DIM = 128
DIFF_TOTAL = DIFF_HEADS * DIFF_V_DIM
LANES = 128
SUBLANES = 8

V7X_VMEM_BYTES = 64 * 1024 * 1024
VMEM_LIMIT = V7X_VMEM_BYTES - 8 * 1024 * 1024

FFN_TM = 1024
FFN_ROW_BLOCKS = 2
FFN_TF = 256
PROJ_TM = 512
RET_TS = 512
CONV_TS = 512
CONV_RB = 64
ATT_TQ = 512
ATT_TK = 512
ATT_ONES_ROWS = 16
ATT_V_ROWS = DIFF_V_DIM + ATT_ONES_ROWS
ATT_SAFE_LOG2 = 50.0
Q_SCALE_LOG2 = DIFF_QK_DIM ** -0.5 * math.log2(math.e)


def _tile(n, pref):
    t = min(n, pref)
    while n % t:
        t //= 2
    return t


def _params(*sem):
    return pltpu.CompilerParams(dimension_semantics=sem, vmem_limit_bytes=VMEM_LIMIT)


def _resident(shape):
    zeros = (0,) * len(shape)
    return pl.BlockSpec(shape, lambda *_: zeros, pipeline_mode=pl.Buffered(1))


def _rms(x, g):
    return x * lax.rsqrt(jnp.mean(x * x, axis=-1, keepdims=True) + EPS) * g


def _silu(x):
    return x * jax.nn.sigmoid(x)


def _mm(a, b):
    return jnp.dot(a, b, preferred_element_type=F32)


def _ffn_kernel(*refs, n_acts, tf):
    x_ref, act_refs, rest = refs[0], refs[1:1 + n_acts], refs[1 + n_acts:]
    wp_ref = rest[0] if n_acts else None
    g_ref, wi_ref, wo_ref, o_ref, xn_ref, acc_ref = rest[1:] if n_acts else rest
    tm = x_ref.shape[0]
    blocks = [slice(r, r + tm // FFN_ROW_BLOCKS) for r in range(0, tm, tm // FFN_ROW_BLOCKS)]
    for rows in blocks:
        x = x_ref[rows, :]
        row = 0
        for a_ref in act_refs:
            x = x + _mm(a_ref[rows, :], wp_ref[row:row + a_ref.shape[1], :])
            row += a_ref.shape[1]
        o_ref[rows, :] = x
        xn_ref[rows, :] = _rms(x, g_ref[...]).astype(BF16)
    acc_ref[...] = jnp.zeros_like(acc_ref)
    ff = wo_ref.shape[0]
    for lo in range(0, ff, tf):
        for rows in blocks:
            xn = xn_ref[rows, :]
            h = _silu(_mm(xn, wi_ref[:, lo:lo + tf])) * _mm(xn, wi_ref[:, ff + lo:ff + lo + tf])
            acc_ref[rows, :] += _mm(h.astype(BF16), wo_ref[lo:lo + tf, :])
    for rows in blocks:
        o_ref[rows, :] += 0.5 * acc_ref[rows, :]


def _ffn(x, acts, w_pre, g, w_in, w_out, layer, half):
    t, d = x.shape
    ff = w_out.shape[2]
    tm = _tile(t, FFN_TM)
    tok = lambda i: (i, 0)
    pick = lambda i: (layer, half, 0, 0)
    return pl.pallas_call(
        functools.partial(_ffn_kernel, n_acts=len(acts), tf=_tile(ff, FFN_TF)),
        grid=(t // tm,),
        in_specs=[pl.BlockSpec((tm, d), tok)]
        + [pl.BlockSpec((tm, a.shape[1]), tok) for a in acts]
        + ([_resident(w_pre.shape)] if acts else [])
        + [
            _resident((1, d)),
            pl.BlockSpec((None, None, d, 2 * ff), pick, pipeline_mode=pl.Buffered(1)),
            pl.BlockSpec((None, None, ff, d), pick, pipeline_mode=pl.Buffered(1)),
        ],
        out_specs=pl.BlockSpec((tm, d), tok),
        out_shape=jax.ShapeDtypeStruct((t, d), F32),
        scratch_shapes=[pltpu.VMEM((tm, d), BF16), pltpu.VMEM((tm, d), F32)],
        compiler_params=_params("parallel"),
        name="ffn",
    )(x, *acts, *([w_pre] if acts else []), g, w_in, w_out)


def _ab_in_kernel(x_ref, g_ref, w_ref, cos_ref, sin_ref, q_ref, k_ref, v_ref, gate_ref, u_ref):
    xn = _rms(x_ref[...], g_ref[...]).astype(BF16)
    cos = cos_ref[...]
    sin = sin_ref[...]

    def proj(seg):
        return _mm(xn, w_ref[:, seg * RET_DIM:(seg + 1) * RET_DIM])

    def rope(y):
        return y * cos + pltpu.roll(y, RET_HEAD_DIM // 2, 1) * sin

    q = proj(0)
    k = proj(1)
    for h in range(RET_HEADS):
        cols = slice(h * RET_HEAD_DIM, (h + 1) * RET_HEAD_DIM)
        q_ref[:, cols] = rope(q[:, cols]).astype(BF16)
        k_ref[:, cols] = (rope(k[:, cols]) * RET_HEAD_DIM ** -0.5).astype(BF16)
    v_ref[...] = proj(2).astype(BF16)
    gate_ref[...] = proj(3)
    u_ref[...] = proj(4) * jax.nn.sigmoid(proj(5))


def _ab_in(x, g, w, cos, sin, seq):
    t, d = x.shape
    tm = _tile(seq, PROJ_TM)
    npos = seq // tm
    tok = lambda i: (i, 0)
    pos = lambda i: (i % npos, 0)
    wide = jax.ShapeDtypeStruct((t, RET_DIM), F32)
    narrow = jax.ShapeDtypeStruct((t, RET_DIM), BF16)
    return pl.pallas_call(
        _ab_in_kernel,
        grid=(t // tm,),
        in_specs=[
            pl.BlockSpec((tm, d), tok),
            _resident((1, d)),
            _resident(w.shape),
            pl.BlockSpec((tm, RET_HEAD_DIM), pos),
            pl.BlockSpec((tm, RET_HEAD_DIM), pos),
        ],
        out_specs=[pl.BlockSpec((tm, RET_DIM), tok)] * 5,
        out_shape=[narrow, narrow, narrow, wide, wide],
        compiler_params=_params("parallel"),
        name="ab_in",
    )(x, g, w, cos, sin)


def _ret_decays(decay_ref, direction, h):
    c = RET_CHUNK
    lg = -jnp.exp(jnp.full((1, 1), decay_ref[direction, h], F32))
    idx = lax.broadcasted_iota(jnp.int32, (c, 1), 0).astype(F32)
    if direction == 0:
        q_dec = jnp.exp(lg * (idx + 1.0))
        k_dec = jnp.exp(lg * (c - 1.0 - idx))
    else:
        q_dec = jnp.exp(lg * (c - idx))
        k_dec = jnp.exp(lg * idx)
    return lg, q_dec, k_dec, jnp.exp(lg * float(c))


def _ret_bwd_kernel(decay_ref, q_ref, k_ref, v_ref, o_ref, state_ref, *, ts):
    c = RET_CHUNK

    @pl.when(pl.program_id(1) == 0)
    def _():
        state_ref[...] = jnp.zeros_like(state_ref)

    dec = [_ret_decays(decay_ref, 1, h) for h in range(RET_HEADS)]
    for ci in reversed(range(ts // c)):
        rows = slice(ci * c, (ci + 1) * c)
        for h in range(RET_HEADS):
            cols = slice(h * RET_HEAD_DIM, (h + 1) * RET_HEAD_DIM)
            _, q_dec, k_dec, c_dec = dec[h]
            qc = q_ref[0, rows, cols].astype(F32)
            kc = k_ref[0, rows, cols].astype(F32)
            vc = v_ref[0, rows, cols]
            state = state_ref[h]
            o_ref[0, rows, cols] = _mm((qc * q_dec).astype(BF16), state.astype(BF16))
            state_ref[h] = state * c_dec + _mm((kc * k_dec).T.astype(BF16), vc)


def _ret_fwd_kernel(decay_ref, q_ref, k_ref, v_ref, ob_ref, gate_ref, gn_ref, o_ref, state_ref, *, ts):
    c = RET_CHUNK

    @pl.when(pl.program_id(1) == 0)
    def _():
        state_ref[...] = jnp.zeros_like(state_ref)

    diff = (lax.broadcasted_iota(jnp.int32, (c, c), 0) - lax.broadcasted_iota(jnp.int32, (c, c), 1)).astype(F32)
    dec = []
    for h in range(RET_HEADS):
        lgf, q_dec, k_dec, c_dec = _ret_decays(decay_ref, 0, h)
        lgb = _ret_decays(decay_ref, 1, h)[0]
        dmat = jnp.where(diff >= 0, jnp.exp(lgf * jnp.maximum(diff, 0.0)), jnp.exp(lgb * jnp.maximum(-diff, 0.0)))
        dec.append((dmat, q_dec, k_dec, c_dec))
    for ci in range(ts // c):
        rows = slice(ci * c, (ci + 1) * c)
        for h in range(RET_HEADS):
            cols = slice(h * RET_HEAD_DIM, (h + 1) * RET_HEAD_DIM)
            dmat, q_dec, k_dec, c_dec = dec[h]
            qb = q_ref[0, rows, cols]
            kb = k_ref[0, rows, cols]
            qc = qb.astype(F32)
            kc = kb.astype(F32)
            vc = v_ref[0, rows, cols]
            state = state_ref[h]
            scores = lax.dot_general(qb, kb, (((1,), (1,)), ((), ())), preferred_element_type=F32) * dmat
            o = _mm(scores.astype(BF16), vc) + _mm((qc * q_dec).astype(BF16), state.astype(BF16))
            o = o + ob_ref[0, rows, cols]
            state_ref[h] = state * c_dec + _mm((kc * k_dec).T.astype(BF16), vc)
            o_ref[0, rows, cols] = (_silu(gate_ref[0, rows, cols]) * _rms(o, gn_ref[:, cols])).astype(BF16)


def _retention(decay, q, k, v, gate, gn):
    b, s, d = q.shape
    ts = _tile(s, RET_TS)
    nt = s // ts
    state = pltpu.VMEM((RET_HEADS, RET_HEAD_DIM, RET_HEAD_DIM), F32)
    fwd = lambda bi, ti: (bi, ti, 0)
    rev = lambda bi, ti: (bi, nt - 1 - ti, 0)
    smem = pl.BlockSpec(memory_space=pltpu.SMEM)
    ob = pl.pallas_call(
        functools.partial(_ret_bwd_kernel, ts=ts),
        grid=(b, nt),
        in_specs=[smem] + [pl.BlockSpec((1, ts, d), rev)] * 3,
        out_specs=pl.BlockSpec((1, ts, d), rev),
        out_shape=jax.ShapeDtypeStruct((b, s, d), F32),
        scratch_shapes=[state],
        compiler_params=_params("parallel", "arbitrary"),
        name="ret_bwd",
    )(decay, q, k, v)
    return pl.pallas_call(
        functools.partial(_ret_fwd_kernel, ts=ts),
        grid=(b, nt),
        in_specs=[smem] + [pl.BlockSpec((1, ts, d), fwd)] * 5 + [_resident(gn.shape)],
        out_specs=pl.BlockSpec((1, ts, d), fwd),
        out_shape=jax.ShapeDtypeStruct((b, s, d), BF16),
        scratch_shapes=[state],
        compiler_params=_params("parallel", "arbitrary"),
        name="ret_fwd",
    )(decay, q, k, v, ob, gate, gn)


def _conv_kernel(u_ref, prev_ref, next_ref, w_ref, b_ref, g_ref, o_ref, buf_ref, *, ts, rb):
    i = pl.program_id(1)
    zero = jnp.zeros((CONV_HALO, CONV_DIM), F32)
    buf_ref[0, 0:CONV_HALO, :] = jnp.where(i > 0, prev_ref[0], zero)
    buf_ref[0, CONV_HALO:CONV_HALO + ts, :] = u_ref[0]
    buf_ref[0, CONV_HALO + ts:, :] = jnp.where(i < pl.num_programs(1) - 1, next_ref[0], zero)
    rows = ts + 2 * CONV_HALO - SUBLANES
    for r in range(1, SUBLANES):
        buf_ref[r, 0:rows, :] = buf_ref[0, r:r + rows, :]
    w = w_ref[...]
    first = CONV_HALO - CONV_PAD
    for r0 in range(0, ts, rb):
        acc = jnp.zeros((rb, CONV_DIM), F32)
        for t in range(CONV_WIDTH):
            shift, base = (r0 + first + t) % SUBLANES, (r0 + first + t) // SUBLANES * SUBLANES
            acc = acc + w[t:t + 1, :] * buf_ref[shift, base:base + rb, :]
        y = _rms(acc + b_ref[...], g_ref[...])
        o_ref[0, r0:r0 + rb, :] = _silu(y).astype(BF16)


def _conv(u, w, bias, g):
    b, s, d = u.shape
    ts = _tile(s, CONV_TS)
    nt = s // ts
    per = ts // CONV_HALO
    nh = s // CONV_HALO
    return pl.pallas_call(
        functools.partial(_conv_kernel, ts=ts, rb=_tile(ts, CONV_RB)),
        grid=(b, nt),
        in_specs=[
            pl.BlockSpec((1, ts, d), lambda bi, ti: (bi, ti, 0)),
            pl.BlockSpec((1, CONV_HALO, d), lambda bi, ti: (bi, jnp.maximum(ti * per - 1, 0), 0)),
            pl.BlockSpec((1, CONV_HALO, d), lambda bi, ti: (bi, jnp.minimum((ti + 1) * per, nh - 1), 0)),
            _resident(w.shape),
            _resident(bias.shape),
            _resident(g.shape),
        ],
        out_specs=pl.BlockSpec((1, ts, d), lambda bi, ti: (bi, ti, 0)),
        out_shape=jax.ShapeDtypeStruct((b, s, d), BF16),
        scratch_shapes=[pltpu.VMEM((SUBLANES, ts + 2 * CONV_HALO, d), F32)],
        compiler_params=_params("parallel", "parallel"),
        name="conv",
    )(u, u, u, w, bias, g)


def _c_in_kernel(x_ref, g_ref, w_ref, qg_ref, kg_ref, cos_ref, sin_ref, qt_ref, k_ref, vt_ref):
    xn = _rms(x_ref[...], g_ref[...]).astype(BF16)
    cos = cos_ref[...]
    sin = sin_ref[...]
    row = lax.broadcasted_iota(jnp.int32, (LANES, LANES), 0) >= DIFF_QK_DIM
    col = lax.broadcasted_iota(jnp.int32, (LANES, LANES), 1) >= DIFF_QK_DIM
    group = jnp.where(row == col, 1.0, 0.0).astype(BF16)
    group = jnp.concatenate([group, group], axis=0)
    lane = lax.broadcasted_iota(jnp.int32, (1, LANES), 1)
    low_half = jnp.bitwise_and(lane, DIFF_QK_DIM // 2) == 0
    wide = 2 * LANES

    def head_pair(y, gain, scale):
        sq = y * y
        hi = sq.astype(BF16)
        lo = (sq - hi.astype(F32)).astype(BF16)
        ms = _mm(jnp.concatenate([hi, lo], axis=1), group) * (1.0 / DIFF_QK_DIM)
        yn = y * lax.rsqrt(ms + EPS) * gain
        half = DIFF_QK_DIM // 2
        partner = jnp.where(low_half, pltpu.roll(yn, LANES - half, 1), pltpu.roll(yn, half, 1))
        return (yn * cos + partner * sin) * scale

    ones = jnp.ones((ATT_ONES_ROWS, x_ref.shape[0]), BF16)
    for cb in range(0, DIFF_TOTAL, wide):
        yq = _mm(xn, w_ref[:, cb:cb + wide])
        yk = _mm(xn, w_ref[:, DIFF_TOTAL + cb:DIFF_TOTAL + cb + wide])
        yv = _mm(xn, w_ref[:, 2 * DIFF_TOTAL + cb:2 * DIFF_TOTAL + cb + wide])
        for sub in range(0, wide, LANES):
            cols = slice(sub, sub + LANES)
            head = (cb + sub) // DIFF_V_DIM
            q = head_pair(yq[:, cols], qg_ref[...], Q_SCALE_LOG2)
            qt_ref[0, cb + sub:cb + sub + LANES, :] = q.T.astype(BF16)
            k_ref[:, cb + sub:cb + sub + LANES] = head_pair(yk[:, cols], kg_ref[...], 1.0).astype(BF16)
            base = head * ATT_V_ROWS
            vt_ref[0, 0, base:base + DIFF_V_DIM, :] = yv[:, cols].T.astype(BF16)
            vt_ref[0, 0, base + DIFF_V_DIM:base + ATT_V_ROWS, :] = ones


def _c_in(x, g, w, qg, kg, cos, sin, seq):
    t, d = x.shape
    tm = _tile(seq, ATT_TK)
    npos = seq // tm
    b = t // seq
    tok = lambda i: (i, 0)
    pos = lambda i: (i % npos, 0)
    vrows = DIFF_HEADS * ATT_V_ROWS
    return pl.pallas_call(
        _c_in_kernel,
        grid=(t // tm,),
        in_specs=[
            pl.BlockSpec((tm, d), tok),
            _resident((1, d)),
            _resident(w.shape),
            _resident(qg.shape),
            _resident(kg.shape),
            pl.BlockSpec((tm, LANES), pos),
            pl.BlockSpec((tm, LANES), pos),
        ],
        out_specs=[
            pl.BlockSpec((1, DIFF_TOTAL, tm), lambda i: (i // npos, 0, i % npos)),
            pl.BlockSpec((tm, DIFF_TOTAL), tok),
            pl.BlockSpec((1, 1, vrows, tm), lambda i: (i // npos, i % npos, 0, 0)),
        ],
        out_shape=[
            jax.ShapeDtypeStruct((b, DIFF_TOTAL, seq), BF16),
            jax.ShapeDtypeStruct((t, DIFF_TOTAL), BF16),
            jax.ShapeDtypeStruct((b, npos, vrows, tm), BF16),
        ],
        compiler_params=_params("parallel"),
        name="c_in",
    )(x, g, w, qg, kg, cos, sin)


def _attn_queries(qt_ref):
    qt = qt_ref[0].astype(F32)
    row = lax.broadcasted_iota(jnp.int32, qt.shape, 0)
    return jnp.concatenate([jnp.where(row < DIFF_QK_DIM, qt, 0.0), jnp.where(row >= DIFF_QK_DIM, qt, 0.0)],
                           axis=1).astype(BF16)


def _attn_finish(lam_ref, sg_ref, num, den, o_ref, *, tq, lam_init):
    lam = lam_ref[...]
    lam_full = (jnp.exp(jnp.sum(lam[0:1] * lam[1:2], axis=1, keepdims=True))
                - jnp.exp(jnp.sum(lam[2:3] * lam[3:4], axis=1, keepdims=True)) + lam_init)
    out = num / den
    o = out[:, :tq] - lam_full * out[:, tq:]
    o = o * lax.rsqrt(jnp.mean(o * o, axis=0, keepdims=True) + EPS)
    o_ref[0] = (o.T * sg_ref[...] * (1.0 - lam_init)).astype(BF16)


def _attn_bounded_kernel(lam_ref, sg_ref, qt_ref, k_ref, vt_ref, o_ref, acc_ref, *, tq, tk, nk, lam_init):
    qs = _attn_queries(qt_ref)
    acc_ref[...] = jnp.zeros_like(acc_ref)

    def body(j, carry):
        kj = k_ref[0, pl.ds(pl.multiple_of(j * tk, tk), tk), :]
        acc_ref[...] += _mm(vt_ref[0, j], jnp.exp2(_mm(kj, qs)).astype(BF16))
        return carry

    lax.fori_loop(0, nk, body, 0, unroll=8)
    acc = acc_ref[...]
    _attn_finish(lam_ref, sg_ref, acc[:DIFF_V_DIM], acc[DIFF_V_DIM:DIFF_V_DIM + 1], o_ref, tq=tq, lam_init=lam_init)


def _attn_online_kernel(lam_ref, sg_ref, qt_ref, k_ref, vt_ref, o_ref, acc_ref, s_ref, *, tq, tk, nk, lam_init):
    qs = _attn_queries(qt_ref)
    acc_ref[...] = jnp.zeros_like(acc_ref)

    def scores(j, slot):
        s = _mm(k_ref[0, pl.ds(pl.multiple_of(j * tk, tk), tk), :], qs)
        s_ref[slot] = s
        return jnp.max(s, axis=0, keepdims=True)

    def step(j, slot, carry, prefetch=True):
        m_prev, tile_max = carry
        next_max = scores(j + 1, 1 - slot) if prefetch else tile_max
        m_new = jnp.maximum(m_prev, tile_max)
        p = jnp.exp2(s_ref[slot] - m_new).astype(BF16)
        acc_ref[...] = jnp.exp2(m_prev - m_new) * acc_ref[...] + _mm(vt_ref[0, j], p)
        return m_new, next_max

    def pair(jj, carry):
        return step(2 * jj + 1, 1, step(2 * jj, 0, carry))

    carry = (jnp.full((1, 2 * tq), -jnp.inf, F32), scores(0, 0))
    carry = lax.fori_loop(0, nk // 2 - 1, pair, carry)
    step(nk - 1, 1, step(nk - 2, 0, carry), prefetch=False)
    acc = acc_ref[...]
    _attn_finish(lam_ref, sg_ref, acc[:DIFF_V_DIM], acc[DIFF_V_DIM:DIFF_V_DIM + 1], o_ref, tq=tq, lam_init=lam_init)


def _diff_attn(lam, sg, qt, k, vt, lam_init, score_bound):
    b, s, d = k.shape
    nk, tk = vt.shape[1], vt.shape[3]
    assert nk % 2 == 0, "key tiles are consumed in pairs"
    tq = _tile(s, ATT_TQ)

    def call(body, name, scratch):
        return pl.pallas_call(
            functools.partial(body, tq=tq, tk=tk, nk=nk, lam_init=lam_init),
            grid=(b, DIFF_HEADS, s // tq),
            in_specs=[
                _resident(lam.shape),
                _resident(sg.shape),
                pl.BlockSpec((1, DIFF_V_DIM, tq), lambda bi, h, qi: (bi, h, qi)),
                pl.BlockSpec((1, s, DIFF_V_DIM), lambda bi, h, qi: (bi, 0, h)),
                pl.BlockSpec((1, nk, ATT_V_ROWS, tk), lambda bi, h, qi: (bi, 0, h, 0)),
            ],
            out_specs=pl.BlockSpec((1, tq, DIFF_V_DIM), lambda bi, h, qi: (bi, qi, h)),
            out_shape=jax.ShapeDtypeStruct((b, s, d), BF16),
            scratch_shapes=scratch,
            compiler_params=_params("parallel", "parallel", "arbitrary"),
            name=name,
        )

    acc = pltpu.VMEM((ATT_V_ROWS, 2 * tq), F32)
    bounded = call(_attn_bounded_kernel, "diff_attn_bounded", [acc])
    online = call(_attn_online_kernel, "diff_attn_online", [acc, pltpu.VMEM((2, tk, 2 * tq), F32)])
    return lax.cond(score_bound <= ATT_SAFE_LOG2, bounded, online, lam, sg, qt, k, vt)


def _rope_tables(seq, head_dim):
    half = head_dim // 2
    inv = ROPE_THETA ** (-jnp.arange(half, dtype=F32) / half)
    ang = jnp.arange(seq, dtype=F32)[:, None] * inv[None, :]
    cos = jnp.cos(ang)
    sin = jnp.sin(ang)
    reps = LANES // head_dim
    return (jnp.tile(jnp.concatenate([cos, cos], axis=-1), (1, reps)),
            jnp.tile(jnp.concatenate([-sin, sin], axis=-1), (1, reps)))


def _lambda_init(layer):
    return 0.8 - 0.6 * math.exp(-0.3 * layer)


def kernel(x_prompt, x_sample, norm_g, ffn_w_in, ffn_w_out, ab_w_in, ab_decay, ab_ret_norm_g, ab_conv_w, ab_conv_b,
           ab_conv_norm_g, ab_w_out, c_w_in, c_q_norm_g, c_k_norm_g, c_lambda, c_subln_g, c_w_out):
    depth = norm_g.shape[0]
    d = norm_g.shape[-1]
    ffn_w_in_b = ffn_w_in.astype(BF16)
    ffn_w_out_b = ffn_w_out.astype(BF16)
    ab_w_in_b = ab_w_in.astype(BF16)
    ab_w_out_b = ab_w_out.astype(BF16)
    c_w_in_b = c_w_in.astype(BF16)
    c_w_out_b = c_w_out.astype(BF16)

    def trunk(x3):
        b, s, _ = x3.shape
        x = x3.reshape(b * s, d)
        seq3 = lambda a: a.reshape(b, s, a.shape[-1])
        flat = lambda a: a.reshape(b * s, a.shape[-1])
        for layer in range(depth):
            g = norm_g[layer][:, None, :]
            x = _ffn(x, [], None, g[0], ffn_w_in_b, ffn_w_out_b, layer, 0)
            i = layer // 2
            if layer % 2 == 0:
                cos, sin = _rope_tables(s, RET_HEAD_DIM)
                q, k, v, gate, u = _ab_in(x, g[1], ab_w_in_b[i], cos, sin, s)
                ret = _retention(ab_decay[i], seq3(q), seq3(k), seq3(v), seq3(gate), ab_ret_norm_g[i][None])
                conv = _conv(seq3(u), ab_conv_w[i], ab_conv_b[i][None], ab_conv_norm_g[i][None])
                acts, w_pre = [flat(ret), flat(conv)], ab_w_out_b[i]
            else:
                cos, sin = _rope_tables(s, DIFF_QK_DIM)
                reps = LANES // DIFF_QK_DIM
                qt, k, vt = _c_in(x, g[1], c_w_in_b[i], jnp.tile(c_q_norm_g[i], reps)[None],
                                  jnp.tile(c_k_norm_g[i], reps)[None], cos, sin, s)
                bound = (DIFF_QK_DIM * Q_SCALE_LOG2 * (1 + 2.0 ** -7) * jnp.max(jnp.abs(c_q_norm_g[i]))
                         * jnp.max(jnp.abs(c_k_norm_g[i])))
                o = _diff_attn(c_lambda[i], c_subln_g[i][None], qt, seq3(k), vt, _lambda_init(layer), bound)
                acts, w_pre = [flat(o)], c_w_out_b[i]
            x = _ffn(x, acts, w_pre, g[2], ffn_w_in_b, ffn_w_out_b, layer, 1)
        return x.reshape(b, s, d)

    return trunk(x_prompt), trunk(x_sample)
```

```python
import functools
import math

import jax
import jax.numpy as jnp
from jax import lax
from jax.experimental import pallas as pl
from jax.experimental.pallas import tpu as pltpu

F32 = jnp.float32
BF16 = jnp.bfloat16

EPS = 1e-6
ROPE_THETA = 10000.0
RET_HEADS = 4
RET_HEAD_DIM = 128
RET_DIM = RET_HEADS * RET_HEAD_DIM
RET_CHUNK = 128
CONV_DIM = 512
CONV_WIDTH = 31
CONV_PAD = CONV_WIDTH // 2
CONV_HALO = 16
DIFF_HEADS = 8
DIFF_QK_DIM = 64
DIFF_V_DIM = 128
DIFF_TOTAL = DIFF_HEADS * DIFF_V_DIM
LANES = 128
SUBLANES = 8

V7X_VMEM_BYTES = 64 * 1024 * 1024
VMEM_LIMIT = V7X_VMEM_BYTES - 8 * 1024 * 1024

FFN_TM = 1024
FFN_ROW_BLOCKS = 2
FFN_TF = 256
PROJ_TM = 512
RET_TS = 512
CONV_RB = 64
ATT_TQ = 512
ATT_TK = 512
ATT_ONES_ROWS = 16
ATT_V_ROWS = DIFF_V_DIM + ATT_ONES_ROWS
ATT_SAFE_LOG2 = 50.0
Q_SCALE_LOG2 = DIFF_QK_DIM ** -0.5 * math.log2(math.e)


def _tile(n, pref):
    t = min(n, pref)
    while n % t:
        t //= 2
    return t


def _params(*sem):
    return pltpu.CompilerParams(dimension_semantics=sem, vmem_limit_bytes=VMEM_LIMIT)


def _resident(shape):
    zeros = (0,) * len(shape)
    return pl.BlockSpec(shape, lambda *_: zeros, pipeline_mode=pl.Buffered(1))


def _rms(x, g):
    return x * lax.rsqrt(jnp.mean(x * x, axis=-1, keepdims=True) + EPS) * g


def _silu(x):
    return x * jax.nn.sigmoid(x)


def _mm(a, b):
    return jnp.dot(a, b, preferred_element_type=F32)


def _ffn_kernel(*refs, n_acts, tf):
    x_ref, act_refs, rest = refs[0], refs[1:1 + n_acts], refs[1 + n_acts:]
    wp_ref = rest[0] if n_acts else None
    g_ref, wi_ref, wo_ref, o_ref, xn_ref, acc_ref = rest[1:] if n_acts else rest
    tm = x_ref.shape[0]
    blocks = [slice(r, r + tm // FFN_ROW_BLOCKS) for r in range(0, tm, tm // FFN_ROW_BLOCKS)]
    for rows in blocks:
        x = x_ref[rows, :]
        row = 0
        for a_ref in act_refs:
            x = x + _mm(a_ref[rows, :], wp_ref[row:row + a_ref.shape[1], :])
            row += a_ref.shape[1]
        o_ref[rows, :] = x
        xn_ref[rows, :] = _rms(x, g_ref[...]).astype(BF16)
    acc_ref[...] = jnp.zeros_like(acc_ref)
    ff = wo_ref.shape[0]
    for lo in range(0, ff, tf):
        for rows in blocks:
            xn = xn_ref[rows, :]
            h = _silu(_mm(xn, wi_ref[:, lo:lo + tf])) * _mm(xn, wi_ref[:, ff + lo:ff + lo + tf])
            acc_ref[rows, :] += _mm(h.astype(BF16), wo_ref[lo:lo + tf, :])
    for rows in blocks:
        o_ref[rows, :] += 0.5 * acc_ref[rows, :]


def _ffn(x, acts, w_pre, g, w_in, w_out, layer, half):
    t, d = x.shape
    ff = w_out.shape[2]
    tm = _tile(t, FFN_TM)
    tok = lambda i: (i, 0)
    pick = lambda i: (layer, half, 0, 0)
    return pl.pallas_call(
        functools.partial(_ffn_kernel, n_acts=len(acts), tf=_tile(ff, FFN_TF)),
        grid=(t // tm,),
        in_specs=[pl.BlockSpec((tm, d), tok)]
        + [pl.BlockSpec((tm, a.shape[1]), tok) for a in acts]
        + ([_resident(w_pre.shape)] if acts else [])
        + [
            _resident((1, d)),
            pl.BlockSpec((None, None, d, 2 * ff), pick, pipeline_mode=pl.Buffered(1)),
            pl.BlockSpec((None, None, ff, d), pick, pipeline_mode=pl.Buffered(1)),
        ],
        out_specs=pl.BlockSpec((tm, d), tok),
        out_shape=jax.ShapeDtypeStruct((t, d), F32),
        scratch_shapes=[pltpu.VMEM((tm, d), BF16), pltpu.VMEM((tm, d), F32)],
        compiler_params=_params("parallel"),
        name="ffn",
    )(x, *acts, *([w_pre] if acts else []), g, w_in, w_out)


def _ab_in_kernel(x_ref, g_ref, w_ref, cos_ref, sin_ref, q_ref, k_ref, v_ref, gate_ref, u_ref):
    xn = _rms(x_ref[...], g_ref[...]).astype(BF16)
    cos = cos_ref[...]
    sin = sin_ref[...]

    def proj(seg):
        return _mm(xn, w_ref[:, seg * RET_DIM:(seg + 1) * RET_DIM])

    def rope(y):
        return y * cos + pltpu.roll(y, RET_HEAD_DIM // 2, 1) * sin

    q = proj(0)
    k = proj(1)
    for h in range(RET_HEADS):
        cols = slice(h * RET_HEAD_DIM, (h + 1) * RET_HEAD_DIM)
        q_ref[:, cols] = rope(q[:, cols]).astype(BF16)
        k_ref[:, cols] = (rope(k[:, cols]) * RET_HEAD_DIM ** -0.5).astype(BF16)
    v_ref[...] = proj(2).astype(BF16)
    gate_ref[...] = proj(3)
    u_ref[...] = proj(4) * jax.nn.sigmoid(proj(5))


def _ab_in(x, g, w, cos, sin, seq):
    t, d = x.shape
    tm = _tile(seq, PROJ_TM)
    npos = seq // tm
    tok = lambda i: (i, 0)
    pos = lambda i: (i % npos, 0)
    wide = jax.ShapeDtypeStruct((t, RET_DIM), F32)
    narrow = jax.ShapeDtypeStruct((t, RET_DIM), BF16)
    return pl.pallas_call(
        _ab_in_kernel,
        grid=(t // tm,),
        in_specs=[
            pl.BlockSpec((tm, d), tok),
            _resident((1, d)),
            _resident(w.shape),
            pl.BlockSpec((tm, RET_HEAD_DIM), pos),
            pl.BlockSpec((tm, RET_HEAD_DIM), pos),
        ],
        out_specs=[pl.BlockSpec((tm, RET_DIM), tok)] * 5,
        out_shape=[narrow, narrow, narrow, wide, wide],
        compiler_params=_params("parallel"),
        name="ab_in",
    )(x, g, w, cos, sin)


def _ret_decays(decay_ref, direction, h):
    c = RET_CHUNK
    lg = -jnp.exp(jnp.full((1, 1), decay_ref[direction, h], F32))
    idx = lax.broadcasted_iota(jnp.int32, (c, 1), 0).astype(F32)
    if direction == 0:
        q_dec = jnp.exp(lg * (idx + 1.0))
        k_dec = jnp.exp(lg * (c - 1.0 - idx))
    else:
        q_dec = jnp.exp(lg * (c - idx))
        k_dec = jnp.exp(lg * idx)
    return lg, q_dec, k_dec, jnp.exp(lg * float(c))


def _ret_bwd_kernel(decay_ref, q_ref, k_ref, v_ref, o_ref, state_ref, *, ts):
    c = RET_CHUNK

    @pl.when(pl.program_id(1) == 0)
    def _():
        state_ref[...] = jnp.zeros_like(state_ref)

    dec = [_ret_decays(decay_ref, 1, h) for h in range(RET_HEADS)]
    for ci in reversed(range(ts // c)):
        rows = slice(ci * c, (ci + 1) * c)
        for h in range(RET_HEADS):
            cols = slice(h * RET_HEAD_DIM, (h + 1) * RET_HEAD_DIM)
            _, q_dec, k_dec, c_dec = dec[h]
            qc = q_ref[0, rows, cols].astype(F32)
            kc = k_ref[0, rows, cols].astype(F32)
            vc = v_ref[0, rows, cols]
            state = state_ref[h]
            o_ref[0, rows, cols] = _mm((qc * q_dec).astype(BF16), state.astype(BF16))
            state_ref[h] = state * c_dec + _mm((kc * k_dec).T.astype(BF16), vc)


def _ret_fwd_kernel(decay_ref, q_ref, k_ref, v_ref, ob_ref, gate_ref, gn_ref, o_ref, state_ref, *, ts):
    c = RET_CHUNK

    @pl.when(pl.program_id(1) == 0)
    def _():
        state_ref[...] = jnp.zeros_like(state_ref)

    diff = (lax.broadcasted_iota(jnp.int32, (c, c), 0) - lax.broadcasted_iota(jnp.int32, (c, c), 1)).astype(F32)
    dec = []
    for h in range(RET_HEADS):
        lgf, q_dec, k_dec, c_dec = _ret_decays(decay_ref, 0, h)
        lgb = _ret_decays(decay_ref, 1, h)[0]
        dmat = jnp.where(diff >= 0, jnp.exp(lgf * jnp.maximum(diff, 0.0)), jnp.exp(lgb * jnp.maximum(-diff, 0.0)))
        dec.append((dmat, q_dec, k_dec, c_dec))
    for ci in range(ts // c):
        rows = slice(ci * c, (ci + 1) * c)
        for h in range(RET_HEADS):
            cols = slice(h * RET_HEAD_DIM, (h + 1) * RET_HEAD_DIM)
            dmat, q_dec, k_dec, c_dec = dec[h]
            qb = q_ref[0, rows, cols]
            kb = k_ref[0, rows, cols]
            qc = qb.astype(F32)
            kc = kb.astype(F32)
            vc = v_ref[0, rows, cols]
            state = state_ref[h]
            scores = lax.dot_general(qb, kb, (((1,), (1,)), ((), ())), preferred_element_type=F32) * dmat
            o = _mm(scores.astype(BF16), vc) + _mm((qc * q_dec).astype(BF16), state.astype(BF16))
            o = o + ob_ref[0, rows, cols]
            state_ref[h] = state * c_dec + _mm((kc * k_dec).T.astype(BF16), vc)
            o_ref[0, rows, cols] = (_silu(gate_ref[0, rows, cols]) * _rms(o, gn_ref[:, cols])).astype(BF16)


def _ret_conv_kernel(decay_ref, q_ref, k_ref, v_ref, ob_ref, gate_ref, gn_ref, u_ref, prev_ref, next_ref,
                     w_ref, b_ref, g_ref, o_ref, state_ref, buf_ref, *, ts, rb):
    _ret_fwd_kernel(decay_ref, q_ref, k_ref, v_ref, ob_ref, gate_ref, gn_ref, o_ref, state_ref, ts=ts)
    _conv_kernel(u_ref, prev_ref, next_ref, w_ref, b_ref, g_ref, o_ref, buf_ref, ts=ts, rb=rb)


def _ab_mix(decay, q, k, v, gate, gn, u, w, bias, g):
    b, s, d = q.shape
    ts = _tile(s, RET_TS)
    nt = s // ts
    per = ts // CONV_HALO
    nh = s // CONV_HALO
    state = pltpu.VMEM((RET_HEADS, RET_HEAD_DIM, RET_HEAD_DIM), F32)
    fwd = lambda bi, ti: (bi, ti, 0)
    rev = lambda bi, ti: (bi, nt - 1 - ti, 0)
    smem = pl.BlockSpec(memory_space=pltpu.SMEM)
    ob = pl.pallas_call(
        functools.partial(_ret_bwd_kernel, ts=ts),
        grid=(b, nt),
        in_specs=[smem] + [pl.BlockSpec((1, ts, d), rev)] * 3,
        out_specs=pl.BlockSpec((1, ts, d), rev),
        out_shape=jax.ShapeDtypeStruct((b, s, d), F32),
        scratch_shapes=[state],
        compiler_params=_params("parallel", "arbitrary"),
        name="ret_bwd",
    )(decay, q, k, v)
    return pl.pallas_call(
        functools.partial(_ret_conv_kernel, ts=ts, rb=_tile(ts, CONV_RB)),
        grid=(b, nt),
        in_specs=[smem] + [pl.BlockSpec((1, ts, d), fwd)] * 5 + [_resident(gn.shape)] + [
            pl.BlockSpec((1, ts, CONV_DIM), fwd),
            pl.BlockSpec((1, CONV_HALO, CONV_DIM), lambda bi, ti: (bi, jnp.maximum(ti * per - 1, 0), 0)),
            pl.BlockSpec((1, CONV_HALO, CONV_DIM), lambda bi, ti: (bi, jnp.minimum((ti + 1) * per, nh - 1), 0)),
            _resident(w.shape),
            _resident(bias.shape),
            _resident(g.shape),
        ],
        out_specs=pl.BlockSpec((1, ts, d + CONV_DIM), fwd),
        out_shape=jax.ShapeDtypeStruct((b, s, d + CONV_DIM), BF16),
        scratch_shapes=[state, pltpu.VMEM((SUBLANES, ts + 2 * CONV_HALO, CONV_DIM), F32)],
        compiler_params=_params("parallel", "arbitrary"),
        name="ret_conv",
    )(decay, q, k, v, ob, gate, gn, u, u, u, w, bias, g)


def _conv_kernel(u_ref, prev_ref, next_ref, w_ref, b_ref, g_ref, o_ref, buf_ref, *, ts, rb):
    i = pl.program_id(1)
    zero = jnp.zeros((CONV_HALO, CONV_DIM), F32)
    buf_ref[0, 0:CONV_HALO, :] = jnp.where(i > 0, prev_ref[0], zero)
    buf_ref[0, CONV_HALO:CONV_HALO + ts, :] = u_ref[0]
    buf_ref[0, CONV_HALO + ts:, :] = jnp.where(i < pl.num_programs(1) - 1, next_ref[0], zero)
    rows = ts + 2 * CONV_HALO - SUBLANES
    for r in range(1, SUBLANES):
        buf_ref[r, 0:rows, :] = buf_ref[0, r:r + rows, :]
    first = CONV_HALO - CONV_PAD

    w = w_ref[...]
    for r0 in range(0, ts, rb):
        acc = jnp.zeros((rb, CONV_DIM), F32)
        for t in range(CONV_WIDTH):
            shift, base = (r0 + first + t) % SUBLANES, (r0 + first + t) // SUBLANES * SUBLANES
            acc = acc + w[t:t + 1, :] * buf_ref[shift, base:base + rb, :]
        y = _rms(acc + b_ref[...], g_ref[...])
        o_ref[0, r0:r0 + rb, RET_DIM:] = _silu(y).astype(BF16)


def _c_in_kernel(x_ref, g_ref, w_ref, qg_ref, kg_ref, cos_ref, sin_ref, qt_ref, k_ref, vt_ref):
    xn = _rms(x_ref[...], g_ref[...]).astype(BF16)
    cos = cos_ref[...]
    sin = sin_ref[...]
    row = lax.broadcasted_iota(jnp.int32, (LANES, LANES), 0) >= DIFF_QK_DIM
    col = lax.broadcasted_iota(jnp.int32, (LANES, LANES), 1) >= DIFF_QK_DIM
    group = jnp.where(row == col, 1.0, 0.0).astype(BF16)
    group = jnp.concatenate([group, group], axis=0)
    lane = lax.broadcasted_iota(jnp.int32, (1, LANES), 1)
    low_half = jnp.bitwise_and(lane, DIFF_QK_DIM // 2) == 0
    wide = 2 * LANES

    def head_pair(y, gain, scale):
        sq = y * y
        hi = sq.astype(BF16)
        lo = (sq - hi.astype(F32)).astype(BF16)
        ms = _mm(jnp.concatenate([hi, lo], axis=1), group) * (1.0 / DIFF_QK_DIM)
        yn = y * lax.rsqrt(ms + EPS) * gain
        half = DIFF_QK_DIM // 2
        partner = jnp.where(low_half, pltpu.roll(yn, LANES - half, 1), pltpu.roll(yn, half, 1))
        return (yn * cos + partner * sin) * scale

    ones = jnp.ones((ATT_ONES_ROWS, x_ref.shape[0]), BF16)
    for cb in range(0, DIFF_TOTAL, wide):
        yq = _mm(xn, w_ref[:, cb:cb + wide])
        yk = _mm(xn, w_ref[:, DIFF_TOTAL + cb:DIFF_TOTAL + cb + wide])
        yv = _mm(xn, w_ref[:, 2 * DIFF_TOTAL + cb:2 * DIFF_TOTAL + cb + wide])
        for sub in range(0, wide, LANES):
            cols = slice(sub, sub + LANES)
            head = (cb + sub) // DIFF_V_DIM
            q = head_pair(yq[:, cols], qg_ref[...], Q_SCALE_LOG2)
            qt_ref[0, cb + sub:cb + sub + LANES, :] = q.T.astype(BF16)
            k_ref[:, cb + sub:cb + sub + LANES] = head_pair(yk[:, cols], kg_ref[...], 1.0).astype(BF16)
            base = head * ATT_V_ROWS
            vt_ref[0, 0, base:base + DIFF_V_DIM, :] = yv[:, cols].T.astype(BF16)
            vt_ref[0, 0, base + DIFF_V_DIM:base + ATT_V_ROWS, :] = ones


def _c_in(x, g, w, qg, kg, cos, sin, seq):
    t, d = x.shape
    tm = _tile(seq, ATT_TK)
    npos = seq // tm
    b = t // seq
    tok = lambda i: (i, 0)
    pos = lambda i: (i % npos, 0)
    vrows = DIFF_HEADS * ATT_V_ROWS
    return pl.pallas_call(
        _c_in_kernel,
        grid=(t // tm,),
        in_specs=[
            pl.BlockSpec((tm, d), tok),
            _resident((1, d)),
            _resident(w.shape),
            _resident(qg.shape),
            _resident(kg.shape),
            pl.BlockSpec((tm, LANES), pos),
            pl.BlockSpec((tm, LANES), pos),
        ],
        out_specs=[
            pl.BlockSpec((1, DIFF_TOTAL, tm), lambda i: (i // npos, 0, i % npos)),
            pl.BlockSpec((tm, DIFF_TOTAL), tok),
            pl.BlockSpec((1, 1, vrows, tm), lambda i: (i // npos, i % npos, 0, 0)),
        ],
        out_shape=[
            jax.ShapeDtypeStruct((b, DIFF_TOTAL, seq), BF16),
            jax.ShapeDtypeStruct((t, DIFF_TOTAL), BF16),
            jax.ShapeDtypeStruct((b, npos, vrows, tm), BF16),
        ],
        compiler_params=_params("parallel"),
        name="c_in",
    )(x, g, w, qg, kg, cos, sin)


def _attn_queries(qt_ref):
    qt = qt_ref[0].astype(F32)
    row = lax.broadcasted_iota(jnp.int32, qt.shape, 0)
    return jnp.concatenate([jnp.where(row < DIFF_QK_DIM, qt, 0.0), jnp.where(row >= DIFF_QK_DIM, qt, 0.0)],
                           axis=1).astype(BF16)


def _attn_finish(lam_ref, sg_ref, num, den, o_ref, *, tq, lam_init):
    lam = lam_ref[...]
    lam_full = (jnp.exp(jnp.sum(lam[0:1] * lam[1:2], axis=1, keepdims=True))
                - jnp.exp(jnp.sum(lam[2:3] * lam[3:4], axis=1, keepdims=True)) + lam_init)
    out = num / den
    o = out[:, :tq] - lam_full * out[:, tq:]
    o = o * lax.rsqrt(jnp.mean(o * o, axis=0, keepdims=True) + EPS)
    o_ref[0] = (o.T * sg_ref[...] * (1.0 - lam_init)).astype(BF16)


def _attn_bounded_kernel(lam_ref, sg_ref, qt_ref, k_ref, vt_ref, o_ref, acc_ref, *, tq, tk, nk, lam_init):
    qs = _attn_queries(qt_ref)
    acc_ref[...] = jnp.zeros_like(acc_ref)

    def body(j, carry):
        kj = k_ref[0, pl.ds(pl.multiple_of(j * tk, tk), tk), :]
        acc_ref[...] += _mm(vt_ref[0, j], jnp.exp2(_mm(kj, qs)).astype(BF16))
        return carry

    lax.fori_loop(0, nk, body, 0, unroll=16)
    acc = acc_ref[...]
    _attn_finish(lam_ref, sg_ref, acc[:DIFF_V_DIM], acc[DIFF_V_DIM:DIFF_V_DIM + 1], o_ref, tq=tq, lam_init=lam_init)


def _attn_online_kernel(lam_ref, sg_ref, qt_ref, k_ref, vt_ref, o_ref, acc_ref, s_ref, *, tq, tk, nk, lam_init):
    qs = _attn_queries(qt_ref)
    acc_ref[...] = jnp.zeros_like(acc_ref)

    def scores(j, slot):
        s = _mm(k_ref[0, pl.ds(pl.multiple_of(j * tk, tk), tk), :], qs)
        s_ref[slot] = s
        return jnp.max(s, axis=0, keepdims=True)

    def step(j, slot, carry, prefetch=True):
        m_prev, tile_max = carry
        next_max = scores(j + 1, 1 - slot) if prefetch else tile_max
        m_new = jnp.maximum(m_prev, tile_max)
        p = jnp.exp2(s_ref[slot] - m_new).astype(BF16)
        acc_ref[...] = jnp.exp2(m_prev - m_new) * acc_ref[...] + _mm(vt_ref[0, j], p)
        return m_new, next_max

    def pair(jj, carry):
        return step(2 * jj + 1, 1, step(2 * jj, 0, carry))

    carry = (jnp.full((1, 2 * tq), -jnp.inf, F32), scores(0, 0))
    carry = lax.fori_loop(0, nk // 2 - 1, pair, carry)
    step(nk - 1, 1, step(nk - 2, 0, carry), prefetch=False)
    acc = acc_ref[...]
    _attn_finish(lam_ref, sg_ref, acc[:DIFF_V_DIM], acc[DIFF_V_DIM:DIFF_V_DIM + 1], o_ref, tq=tq, lam_init=lam_init)


def _diff_attn(lam, sg, qt, k, vt, lam_init, score_bound):
    b, s, d = k.shape
    nk, tk = vt.shape[1], vt.shape[3]
    assert nk % 2 == 0, "key tiles are consumed in pairs"
    tq = _tile(s, ATT_TQ)

    def call(body, name, scratch):
        return pl.pallas_call(
            functools.partial(body, tq=tq, tk=tk, nk=nk, lam_init=lam_init),
            grid=(b, DIFF_HEADS, s // tq),
            in_specs=[
                _resident(lam.shape),
                _resident(sg.shape),
                pl.BlockSpec((1, DIFF_V_DIM, tq), lambda bi, h, qi: (bi, h, qi)),
                pl.BlockSpec((1, s, DIFF_V_DIM), lambda bi, h, qi: (bi, 0, h)),
                pl.BlockSpec((1, nk, ATT_V_ROWS, tk), lambda bi, h, qi: (bi, 0, h, 0)),
            ],
            out_specs=pl.BlockSpec((1, tq, DIFF_V_DIM), lambda bi, h, qi: (bi, qi, h)),
            out_shape=jax.ShapeDtypeStruct((b, s, d), BF16),
            scratch_shapes=scratch,
            compiler_params=_params("parallel", "parallel", "arbitrary"),
            name=name,
        )

    acc = pltpu.VMEM((ATT_V_ROWS, 2 * tq), F32)
    bounded = call(_attn_bounded_kernel, "diff_attn_bounded", [acc])
    online = call(_attn_online_kernel, "diff_attn_online", [acc, pltpu.VMEM((2, tk, 2 * tq), F32)])
    return lax.cond(score_bound <= ATT_SAFE_LOG2, bounded, online, lam, sg, qt, k, vt)


def _rope_tables(seq, head_dim):
    half = head_dim // 2
    inv = ROPE_THETA ** (-jnp.arange(half, dtype=F32) / half)
    ang = jnp.arange(seq, dtype=F32)[:, None] * inv[None, :]
    cos = jnp.cos(ang)
    sin = jnp.sin(ang)
    reps = LANES // head_dim
    return (jnp.tile(jnp.concatenate([cos, cos], axis=-1), (1, reps)),
            jnp.tile(jnp.concatenate([-sin, sin], axis=-1), (1, reps)))


def _lambda_init(layer):
    return 0.8 - 0.6 * math.exp(-0.3 * layer)


def kernel(x_prompt, x_sample, norm_g, ffn_w_in, ffn_w_out, ab_w_in, ab_decay, ab_ret_norm_g, ab_conv_w, ab_conv_b,
           ab_conv_norm_g, ab_w_out, c_w_in, c_q_norm_g, c_k_norm_g, c_lambda, c_subln_g, c_w_out):
    depth = norm_g.shape[0]
    d = norm_g.shape[-1]
    ffn_w_in_b = ffn_w_in.astype(BF16)
    ffn_w_out_b = ffn_w_out.astype(BF16)
    ab_w_in_b = ab_w_in.astype(BF16)
    ab_w_out_b = ab_w_out.astype(BF16)
    c_w_in_b = c_w_in.astype(BF16)
    c_w_out_b = c_w_out.astype(BF16)

    def trunk(x3):
        b, s, _ = x3.shape
        x = x3.reshape(b * s, d)
        seq3 = lambda a: a.reshape(b, s, a.shape[-1])
        flat = lambda a: a.reshape(b * s, a.shape[-1])
        for layer in range(depth):
            g = norm_g[layer][:, None, :]
            x = _ffn(x, [], None, g[0], ffn_w_in_b, ffn_w_out_b, layer, 0)
            i = layer // 2
            if layer % 2 == 0:
                cos, sin = _rope_tables(s, RET_HEAD_DIM)
                q, k, v, gate, u = _ab_in(x, g[1], ab_w_in_b[i], cos, sin, s)
                mixed = _ab_mix(ab_decay[i], seq3(q), seq3(k), seq3(v), seq3(gate), ab_ret_norm_g[i][None],
                                seq3(u), ab_conv_w[i], ab_conv_b[i][None], ab_conv_norm_g[i][None])
                acts, w_pre = [flat(mixed)], ab_w_out_b[i]
            else:
                cos, sin = _rope_tables(s, DIFF_QK_DIM)
                reps = LANES // DIFF_QK_DIM
                qt, k, vt = _c_in(x, g[1], c_w_in_b[i], jnp.tile(c_q_norm_g[i], reps)[None],
                                  jnp.tile(c_k_norm_g[i], reps)[None], cos, sin, s)
                bound = (DIFF_QK_DIM * Q_SCALE_LOG2 * (1 + 2.0 ** -7) * jnp.max(jnp.abs(c_q_norm_g[i]))
                         * jnp.max(jnp.abs(c_k_norm_g[i])))
                o = _diff_attn(c_lambda[i], c_subln_g[i][None], qt, seq3(k), vt, _lambda_init(layer), bound)
                acts, w_pre = [flat(o)], c_w_out_b[i]
            x = _ffn(x, acts, w_pre, g[2], ffn_w_in_b, ffn_w_out_b, layer, 1)
        return x.reshape(b, s, d)

    return trunk(x_prompt), trunk(x_sample)
```

```python
import functools
import math

import jax
import jax.numpy as jnp
from jax import lax
from jax.experimental import pallas as pl
from jax.experimental.pallas import tpu as pltpu

F32 = jnp.float32
BF16 = jnp.bfloat16

EPS = 1e-6
ROPE_THETA = 10000.0
RET_HEADS = 4
RET_HEAD_DIM = 128
RET_DIM = RET_HEADS * RET_HEAD_DIM
RET_CHUNK = 128
CONV_DIM = 512
CONV_WIDTH = 31
CONV_PAD = CONV_WIDTH // 2
CONV_HALO = 16
DIFF_HEADS = 8
DIFF_QK_DIM = 64
DIFF_V_DIM = 128
DIFF_TOTAL = DIFF_HEADS * DIFF_V_DIM
LANES = 128
SUBLANES = 8

V7X_VMEM_BYTES = 64 * 1024 * 1024
VMEM_LIMIT = V7X_VMEM_BYTES - 8 * 1024 * 1024

FFN_TM = 1024
FFN_ROW_BLOCKS = 2
FFN_TF = 256
PROJ_TM = 512
RET_TS = 512
CONV_RB = 64
ATT_TQ = 512
ATT_TK = 512
ATT_ONES_ROWS = 16
ATT_V_ROWS = DIFF_V_DIM + ATT_ONES_ROWS
ATT_SAFE_LOG2 = 50.0
Q_SCALE_LOG2 = DIFF_QK_DIM ** -0.5 * math.log2(math.e)


def _tile(n, pref):
    t = min(n, pref)
    while n % t:
        t //= 2
    return t


def _params(*sem):
    return pltpu.CompilerParams(dimension_semantics=sem, vmem_limit_bytes=VMEM_LIMIT)


def _resident(shape):
    zeros = (0,) * len(shape)
    return pl.BlockSpec(shape, lambda *_: zeros, pipeline_mode=pl.Buffered(1))


def _rms(x, g):
    return x * lax.rsqrt(jnp.mean(x * x, axis=-1, keepdims=True) + EPS) * g


def _silu(x):
    return x * jax.nn.sigmoid(x)


def _mm(a, b):
    return jnp.dot(a, b, preferred_element_type=F32)


def _ffn_kernel(*refs, n_acts, tf):
    x_ref, act_refs, rest = refs[0], refs[1:1 + n_acts], refs[1 + n_acts:]
    wp_ref = rest[0] if n_acts else None
    g_ref, wi_ref, wo_ref, o_ref, xn_ref, acc_ref = rest[1:] if n_acts else rest
    tm = x_ref.shape[0]
    blocks = [slice(r, r + tm // FFN_ROW_BLOCKS) for r in range(0, tm, tm // FFN_ROW_BLOCKS)]
    for rows in blocks:
        x = x_ref[rows, :]
        row = 0
        for a_ref in act_refs:
            x = x + _mm(a_ref[rows, :], wp_ref[row:row + a_ref.shape[1], :])
            row += a_ref.shape[1]
        o_ref[rows, :] = x
        xn_ref[rows, :] = _rms(x, g_ref[...]).astype(BF16)
    acc_ref[...] = jnp.zeros_like(acc_ref)
    ff = wo_ref.shape[0]
    for lo in range(0, ff, tf):
        for rows in blocks:
            xn = xn_ref[rows, :]
            h = _silu(_mm(xn, wi_ref[:, lo:lo + tf])) * _mm(xn, wi_ref[:, ff + lo:ff + lo + tf])
            acc_ref[rows, :] += _mm(h.astype(BF16), wo_ref[lo:lo + tf, :])
    for rows in blocks:
        o_ref[rows, :] += 0.5 * acc_ref[rows, :]


def _ffn(x, acts, w_pre, g, w_in, w_out, layer, half):
    t, d = x.shape
    ff = w_out.shape[2]
    tm = _tile(t, FFN_TM)
    tok = lambda i: (i, 0)
    pick = lambda i: (layer, half, 0, 0)
    return pl.pallas_call(
        functools.partial(_ffn_kernel, n_acts=len(acts), tf=_tile(ff, FFN_TF)),
        grid=(t // tm,),
        in_specs=[pl.BlockSpec((tm, d), tok)]
        + [pl.BlockSpec((tm, a.shape[1]), tok) for a in acts]
        + ([_resident(w_pre.shape)] if acts else [])
        + [
            _resident((1, d)),
            pl.BlockSpec((None, None, d, 2 * ff), pick, pipeline_mode=pl.Buffered(1)),
            pl.BlockSpec((None, None, ff, d), pick, pipeline_mode=pl.Buffered(1)),
        ],
        out_specs=pl.BlockSpec((tm, d), tok),
        out_shape=jax.ShapeDtypeStruct((t, d), F32),
        scratch_shapes=[pltpu.VMEM((tm, d), BF16), pltpu.VMEM((tm, d), F32)],
        compiler_params=_params("parallel"),
        name="ffn",
    )(x, *acts, *([w_pre] if acts else []), g, w_in, w_out)


def _ab_in_kernel(x_ref, g_ref, w_ref, cos_ref, sin_ref, q_ref, k_ref, v_ref, gate_ref, u_ref):
    xn = _rms(x_ref[...], g_ref[...]).astype(BF16)
    cos = cos_ref[...]
    sin = sin_ref[...]

    def proj(seg):
        return _mm(xn, w_ref[:, seg * RET_DIM:(seg + 1) * RET_DIM])

    def rope(y):
        return y * cos + pltpu.roll(y, RET_HEAD_DIM // 2, 1) * sin

    q = proj(0)
    k = proj(1)
    for h in range(RET_HEADS):
        cols = slice(h * RET_HEAD_DIM, (h + 1) * RET_HEAD_DIM)
        q_ref[:, cols] = rope(q[:, cols]).astype(BF16)
        k_ref[:, cols] = (rope(k[:, cols]) * RET_HEAD_DIM ** -0.5).astype(BF16)
    v_ref[...] = proj(2).astype(BF16)
    gate_ref[...] = proj(3)
    u_ref[...] = proj(4) * jax.nn.sigmoid(proj(5))


def _ab_in(x, g, w, cos, sin, seq):
    t, d = x.shape
    tm = _tile(seq, PROJ_TM)
    npos = seq // tm
    tok = lambda i: (i, 0)
    pos = lambda i: (i % npos, 0)
    wide = jax.ShapeDtypeStruct((t, RET_DIM), F32)
    narrow = jax.ShapeDtypeStruct((t, RET_DIM), BF16)
    return pl.pallas_call(
        _ab_in_kernel,
        grid=(t // tm,),
        in_specs=[
            pl.BlockSpec((tm, d), tok),
            _resident((1, d)),
            _resident(w.shape),
            pl.BlockSpec((tm, RET_HEAD_DIM), pos),
            pl.BlockSpec((tm, RET_HEAD_DIM), pos),
        ],
        out_specs=[pl.BlockSpec((tm, RET_DIM), tok)] * 5,
        out_shape=[narrow, narrow, narrow, wide, wide],
        compiler_params=_params("parallel"),
        name="ab_in",
    )(x, g, w, cos, sin)


def _ret_decays(decay_ref, direction, h):
    c = RET_CHUNK
    lg = -jnp.exp(jnp.full((1, 1), decay_ref[direction, h], F32))
    idx = lax.broadcasted_iota(jnp.int32, (c, 1), 0).astype(F32)
    if direction == 0:
        q_dec = jnp.exp(lg * (idx + 1.0))
        k_dec = jnp.exp(lg * (c - 1.0 - idx))
    else:
        q_dec = jnp.exp(lg * (c - idx))
        k_dec = jnp.exp(lg * idx)
    return lg, q_dec, k_dec, jnp.exp(lg * float(c))


def _ret_bwd_kernel(decay_ref, q_ref, k_ref, v_ref, o_ref, state_ref, *, ts):
    c = RET_CHUNK

    @pl.when(pl.program_id(1) == 0)
    def _():
        state_ref[...] = jnp.zeros_like(state_ref)

    dec = [_ret_decays(decay_ref, 1, h) for h in range(RET_HEADS)]
    for ci in reversed(range(ts // c)):
        rows = slice(ci * c, (ci + 1) * c)
        for h in range(RET_HEADS):
            cols = slice(h * RET_HEAD_DIM, (h + 1) * RET_HEAD_DIM)
            _, q_dec, k_dec, c_dec = dec[h]
            qc = q_ref[0, rows, cols].astype(F32)
            kc = k_ref[0, rows, cols].astype(F32)
            vc = v_ref[0, rows, cols]
            state = state_ref[h]
            o_ref[0, rows, cols] = _mm((qc * q_dec).astype(BF16), state.astype(BF16))
            state_ref[h] = state * c_dec + _mm((kc * k_dec).T.astype(BF16), vc)


def _ret_fwd_kernel(decay_ref, q_ref, k_ref, v_ref, ob_ref, gate_ref, gn_ref, o_ref, state_ref, *, ts):
    c = RET_CHUNK

    @pl.when(pl.program_id(1) == 0)
    def _():
        state_ref[...] = jnp.zeros_like(state_ref)

    diff = (lax.broadcasted_iota(jnp.int32, (c, c), 0) - lax.broadcasted_iota(jnp.int32, (c, c), 1)).astype(F32)
    dec = []
    for h in range(RET_HEADS):
        lgf, q_dec, k_dec, c_dec = _ret_decays(decay_ref, 0, h)
        lgb = _ret_decays(decay_ref, 1, h)[0]
        dmat = jnp.where(diff >= 0, jnp.exp(lgf * jnp.maximum(diff, 0.0)), jnp.exp(lgb * jnp.maximum(-diff, 0.0)))
        dec.append((dmat, q_dec, k_dec, c_dec))
    for ci in range(ts // c):
        rows = slice(ci * c, (ci + 1) * c)
        for h in range(RET_HEADS):
            cols = slice(h * RET_HEAD_DIM, (h + 1) * RET_HEAD_DIM)
            dmat, q_dec, k_dec, c_dec = dec[h]
            qb = q_ref[0, rows, cols]
            kb = k_ref[0, rows, cols]
            qc = qb.astype(F32)
            kc = kb.astype(F32)
            vc = v_ref[0, rows, cols]
            state = state_ref[h]
            scores = lax.dot_general(qb, kb, (((1,), (1,)), ((), ())), preferred_element_type=F32) * dmat
            o = _mm(scores.astype(BF16), vc) + _mm((qc * q_dec).astype(BF16), state.astype(BF16))
            o = o + ob_ref[0, rows, cols]
            state_ref[h] = state * c_dec + _mm((kc * k_dec).T.astype(BF16), vc)
            o_ref[0, rows, cols] = (_silu(gate_ref[0, rows, cols]) * _rms(o, gn_ref[:, cols])).astype(BF16)


def _ret_conv_kernel(decay_ref, q_ref, k_ref, v_ref, ob_ref, gate_ref, gn_ref, u_ref, prev_ref, next_ref,
                     w_ref, b_ref, g_ref, o_ref, state_ref, buf_ref, *, ts, rb):
    _ret_fwd_kernel(decay_ref, q_ref, k_ref, v_ref, ob_ref, gate_ref, gn_ref, o_ref, state_ref, ts=ts)
    _conv_kernel(u_ref, prev_ref, next_ref, w_ref, b_ref, g_ref, o_ref, buf_ref, ts=ts, rb=rb)


def _ab_mix(decay, q, k, v, gate, gn, u, w, bias, g):
    b, s, d = q.shape
    ts = _tile(s, RET_TS)
    nt = s // ts
    per = ts // CONV_HALO
    nh = s // CONV_HALO
    state = pltpu.VMEM((RET_HEADS, RET_HEAD_DIM, RET_HEAD_DIM), F32)
    fwd = lambda bi, ti: (bi, ti, 0)
    rev = lambda bi, ti: (bi, nt - 1 - ti, 0)
    smem = pl.BlockSpec(memory_space=pltpu.SMEM)
    ob = pl.pallas_call(
        functools.partial(_ret_bwd_kernel, ts=ts),
        grid=(b, nt),
        in_specs=[smem] + [pl.BlockSpec((1, ts, d), rev)] * 3,
        out_specs=pl.BlockSpec((1, ts, d), rev),
        out_shape=jax.ShapeDtypeStruct((b, s, d), F32),
        scratch_shapes=[state],
        compiler_params=_params("parallel", "arbitrary"),
        name="ret_bwd",
    )(decay, q, k, v)
    return pl.pallas_call(
        functools.partial(_ret_conv_kernel, ts=ts, rb=_tile(ts, CONV_RB)),
        grid=(b, nt),
        in_specs=[smem] + [pl.BlockSpec((1, ts, d), fwd)] * 5 + [_resident(gn.shape)] + [
            pl.BlockSpec((1, ts, CONV_DIM), fwd),
            pl.BlockSpec((1, CONV_HALO, CONV_DIM), lambda bi, ti: (bi, jnp.maximum(ti * per - 1, 0), 0)),
            pl.BlockSpec((1, CONV_HALO, CONV_DIM), lambda bi, ti: (bi, jnp.minimum((ti + 1) * per, nh - 1), 0)),
            _resident(w.shape),
            _resident(bias.shape),
            _resident(g.shape),
        ],
        out_specs=pl.BlockSpec((1, ts, d + CONV_DIM), fwd),
        out_shape=jax.ShapeDtypeStruct((b, s, d + CONV_DIM), BF16),
        scratch_shapes=[state, pltpu.VMEM((SUBLANES, ts + 2 * CONV_HALO, CONV_DIM), F32)],
        compiler_params=_params("parallel", "arbitrary"),
        name="ret_conv",
    )(decay, q, k, v, ob, gate, gn, u, u, u, w, bias, g)


def _conv_kernel(u_ref, prev_ref, next_ref, w_ref, b_ref, g_ref, o_ref, buf_ref, *, ts, rb):
    i = pl.program_id(1)
    zero = jnp.zeros((CONV_HALO, CONV_DIM), F32)
    buf_ref[0, 0:CONV_HALO, :] = jnp.where(i > 0, prev_ref[0], zero)
    buf_ref[0, CONV_HALO:CONV_HALO + ts, :] = u_ref[0]
    buf_ref[0, CONV_HALO + ts:, :] = jnp.where(i < pl.num_programs(1) - 1, next_ref[0], zero)
    rows = ts + 2 * CONV_HALO - SUBLANES
    for r in range(1, SUBLANES):
        buf_ref[r, 0:rows, :] = buf_ref[0, r:r + rows, :]
    first = CONV_HALO - CONV_PAD

    w = w_ref[...]
    for r0 in range(0, ts, rb):
        acc = jnp.zeros((rb, CONV_DIM), F32)
        for t in range(CONV_WIDTH):
            shift, base = (r0 + first + t) % SUBLANES, (r0 + first + t) // SUBLANES * SUBLANES
            acc = acc + w[t:t + 1, :] * buf_ref[shift, base:base + rb, :]
        y = _rms(acc + b_ref[...], g_ref[...])
        o_ref[0, r0:r0 + rb, RET_DIM:] = _silu(y).astype(BF16)


def _c_in_kernel(x_ref, g_ref, w_ref, qg_ref, kg_ref, cos_ref, sin_ref, qt_ref, k_ref, vt_ref):
    xn = _rms(x_ref[...], g_ref[...]).astype(BF16)
    cos = cos_ref[...]
    sin = sin_ref[...]
    row = lax.broadcasted_iota(jnp.int32, (LANES, LANES), 0) >= DIFF_QK_DIM
    col = lax.broadcasted_iota(jnp.int32, (LANES, LANES), 1) >= DIFF_QK_DIM
    group = jnp.where(row == col, 1.0, 0.0).astype(BF16)
    group = jnp.concatenate([group, group], axis=0)
    lane = lax.broadcasted_iota(jnp.int32, (1, LANES), 1)
    low_half = jnp.bitwise_and(lane, DIFF_QK_DIM // 2) == 0
    wide = 2 * LANES

    def head_pair(y, gain, scale):
        sq = y * y
        hi = sq.astype(BF16)
        lo = (sq - hi.astype(F32)).astype(BF16)
        ms = _mm(jnp.concatenate([hi, lo], axis=1), group) * (1.0 / DIFF_QK_DIM)
        yn = y * lax.rsqrt(ms + EPS) * gain
        half = DIFF_QK_DIM // 2
        partner = jnp.where(low_half, pltpu.roll(yn, LANES - half, 1), pltpu.roll(yn, half, 1))
        return (yn * cos + partner * sin) * scale

    ones = jnp.ones((ATT_ONES_ROWS, x_ref.shape[0]), BF16)
    for cb in range(0, DIFF_TOTAL, wide):
        yq = _mm(xn, w_ref[:, cb:cb + wide])
        yk = _mm(xn, w_ref[:, DIFF_TOTAL + cb:DIFF_TOTAL + cb + wide])
        yv = _mm(xn, w_ref[:, 2 * DIFF_TOTAL + cb:2 * DIFF_TOTAL + cb + wide])
        for sub in range(0, wide, LANES):
            cols = slice(sub, sub + LANES)
            head = (cb + sub) // DIFF_V_DIM
            q = head_pair(yq[:, cols], qg_ref[...], Q_SCALE_LOG2)
            qt_ref[0, cb + sub:cb + sub + LANES, :] = q.T.astype(BF16)
            k_ref[0, head] = head_pair(yk[:, cols], kg_ref[...], 1.0).astype(BF16)
            base = head * ATT_V_ROWS
            vt_ref[0, 0, base:base + DIFF_V_DIM, :] = yv[:, cols].T.astype(BF16)
            vt_ref[0, 0, base + DIFF_V_DIM:base + ATT_V_ROWS, :] = ones


def _c_in(x, g, w, qg, kg, cos, sin, seq):
    t, d = x.shape
    tm = _tile(seq, ATT_TK)
    npos = seq // tm
    b = t // seq
    tok = lambda i: (i, 0)
    pos = lambda i: (i % npos, 0)
    vrows = DIFF_HEADS * ATT_V_ROWS
    return pl.pallas_call(
        _c_in_kernel,
        grid=(t // tm,),
        in_specs=[
            pl.BlockSpec((tm, d), tok),
            _resident((1, d)),
            _resident(w.shape),
            _resident(qg.shape),
            _resident(kg.shape),
            pl.BlockSpec((tm, LANES), pos),
            pl.BlockSpec((tm, LANES), pos),
        ],
        out_specs=[
            pl.BlockSpec((1, DIFF_TOTAL, tm), lambda i: (i // npos, 0, i % npos)),
            pl.BlockSpec((1, DIFF_HEADS, tm, DIFF_V_DIM), lambda i: (i // npos, 0, i % npos, 0)),
            pl.BlockSpec((1, 1, vrows, tm), lambda i: (i // npos, i % npos, 0, 0)),
        ],
        out_shape=[
            jax.ShapeDtypeStruct((b, DIFF_TOTAL, seq), BF16),
            jax.ShapeDtypeStruct((b, DIFF_HEADS, seq, DIFF_V_DIM), BF16),
            jax.ShapeDtypeStruct((b, npos, vrows, tm), BF16),
        ],
        compiler_params=_params("parallel"),
        name="c_in",
    )(x, g, w, qg, kg, cos, sin)


def _attn_queries(qt_ref):
    qt = qt_ref[0].astype(F32)
    row = lax.broadcasted_iota(jnp.int32, qt.shape, 0)
    return jnp.concatenate([jnp.where(row < DIFF_QK_DIM, qt, 0.0), jnp.where(row >= DIFF_QK_DIM, qt, 0.0)],
                           axis=1).astype(BF16)


def _attn_finish(lam_ref, sg_ref, num, den, o_ref, *, tq, lam_init):
    lam = lam_ref[...]
    lam_full = (jnp.exp(jnp.sum(lam[0:1] * lam[1:2], axis=1, keepdims=True))
                - jnp.exp(jnp.sum(lam[2:3] * lam[3:4], axis=1, keepdims=True)) + lam_init)
    out = num / den
    o = out[:, :tq] - lam_full * out[:, tq:]
    o = o * lax.rsqrt(jnp.mean(o * o, axis=0, keepdims=True) + EPS)
    o_ref[0] = (o.T * sg_ref[...] * (1.0 - lam_init)).astype(BF16)


def _attn_bounded_kernel(lam_ref, sg_ref, qt_ref, k_ref, vt_ref, o_ref, acc_ref, *, tq, tk, nk, lam_init):
    qs = _attn_queries(qt_ref)
    acc_ref[...] = jnp.zeros_like(acc_ref)

    def body(j, carry):
        kj = k_ref[0, 0, pl.ds(pl.multiple_of(j * tk, tk), tk), :]
        acc_ref[...] += _mm(vt_ref[0, j], jnp.exp2(_mm(kj, qs)).astype(BF16))
        return carry

    lax.fori_loop(0, nk, body, 0, unroll=16)
    acc = acc_ref[...]
    _attn_finish(lam_ref, sg_ref, acc[:DIFF_V_DIM], acc[DIFF_V_DIM:DIFF_V_DIM + 1], o_ref, tq=tq, lam_init=lam_init)


def _attn_online_kernel(lam_ref, sg_ref, qt_ref, k_ref, vt_ref, o_ref, acc_ref, s_ref, *, tq, tk, nk, lam_init):
    qs = _attn_queries(qt_ref)
    acc_ref[...] = jnp.zeros_like(acc_ref)

    def scores(j, slot):
        s = _mm(k_ref[0, 0, pl.ds(pl.multiple_of(j * tk, tk), tk), :], qs)
        s_ref[slot] = s
        return jnp.max(s, axis=0, keepdims=True)

    def step(j, slot, carry, prefetch=True):
        m_prev, tile_max = carry
        next_max = scores(j + 1, 1 - slot) if prefetch else tile_max
        m_new = jnp.maximum(m_prev, tile_max)
        p = jnp.exp2(s_ref[slot] - m_new).astype(BF16)
        acc_ref[...] = jnp.exp2(m_prev - m_new) * acc_ref[...] + _mm(vt_ref[0, j], p)
        return m_new, next_max

    def pair(jj, carry):
        return step(2 * jj + 1, 1, step(2 * jj, 0, carry))

    carry = (jnp.full((1, 2 * tq), -jnp.inf, F32), scores(0, 0))
    carry = lax.fori_loop(0, nk // 2 - 1, pair, carry)
    step(nk - 1, 1, step(nk - 2, 0, carry), prefetch=False)
    acc = acc_ref[...]
    _attn_finish(lam_ref, sg_ref, acc[:DIFF_V_DIM], acc[DIFF_V_DIM:DIFF_V_DIM + 1], o_ref, tq=tq, lam_init=lam_init)


def _diff_attn(lam, sg, qt, k, vt, lam_init, score_bound):
    b, _, s, _ = k.shape
    d = DIFF_TOTAL
    nk, tk = vt.shape[1], vt.shape[3]
    assert nk % 2 == 0, "key tiles are consumed in pairs"
    tq = _tile(s, ATT_TQ)

    def call(body, name, scratch):
        return pl.pallas_call(
            functools.partial(body, tq=tq, tk=tk, nk=nk, lam_init=lam_init),
            grid=(b, DIFF_HEADS, s // tq),
            in_specs=[
                _resident(lam.shape),
                _resident(sg.shape),
                pl.BlockSpec((1, DIFF_V_DIM, tq), lambda bi, h, qi: (bi, h, qi)),
                pl.BlockSpec((1, 1, s, DIFF_V_DIM), lambda bi, h, qi: (bi, h, 0, 0)),
                pl.BlockSpec((1, nk, ATT_V_ROWS, tk), lambda bi, h, qi: (bi, 0, h, 0)),
            ],
            out_specs=pl.BlockSpec((1, tq, DIFF_V_DIM), lambda bi, h, qi: (bi, qi, h)),
            out_shape=jax.ShapeDtypeStruct((b, s, d), BF16),
            scratch_shapes=scratch,
            compiler_params=_params("parallel", "parallel", "arbitrary"),
            name=name,
        )

    acc = pltpu.VMEM((ATT_V_ROWS, 2 * tq), F32)
    bounded = call(_attn_bounded_kernel, "diff_attn_bounded", [acc])
    online = call(_attn_online_kernel, "diff_attn_online", [acc, pltpu.VMEM((2, tk, 2 * tq), F32)])
    return lax.cond(score_bound <= ATT_SAFE_LOG2, bounded, online, lam, sg, qt, k, vt)


def _rope_tables(seq, head_dim):
    half = head_dim // 2
    inv = ROPE_THETA ** (-jnp.arange(half, dtype=F32) / half)
    ang = jnp.arange(seq, dtype=F32)[:, None] * inv[None, :]
    cos = jnp.cos(ang)
    sin = jnp.sin(ang)
    reps = LANES // head_dim
    return (jnp.tile(jnp.concatenate([cos, cos], axis=-1), (1, reps)),
            jnp.tile(jnp.concatenate([-sin, sin], axis=-1), (1, reps)))


def _lambda_init(layer):
    return 0.8 - 0.6 * math.exp(-0.3 * layer)


def kernel(x_prompt, x_sample, norm_g, ffn_w_in, ffn_w_out, ab_w_in, ab_decay, ab_ret_norm_g, ab_conv_w, ab_conv_b,
           ab_conv_norm_g, ab_w_out, c_w_in, c_q_norm_g, c_k_norm_g, c_lambda, c_subln_g, c_w_out):
    depth = norm_g.shape[0]
    d = norm_g.shape[-1]
    ffn_w_in_b = ffn_w_in.astype(BF16)
    ffn_w_out_b = ffn_w_out.astype(BF16)
    ab_w_in_b = ab_w_in.astype(BF16)
    ab_w_out_b = ab_w_out.astype(BF16)
    c_w_in_b = c_w_in.astype(BF16)
    c_w_out_b = c_w_out.astype(BF16)
    max_seq = max(x_prompt.shape[1], x_sample.shape[1])
    rope = {dim: _rope_tables(max_seq, dim) for dim in (RET_HEAD_DIM, DIFF_QK_DIM)}

    def trunk(x3):
        b, s, _ = x3.shape
        x = x3.reshape(b * s, d)
        seq3 = lambda a: a.reshape(b, s, a.shape[-1])
        flat = lambda a: a.reshape(b * s, a.shape[-1])
        for layer in range(depth):
            g = norm_g[layer][:, None, :]
            x = _ffn(x, [], None, g[0], ffn_w_in_b, ffn_w_out_b, layer, 0)
            i = layer // 2
            if layer % 2 == 0:
                cos, sin = rope[RET_HEAD_DIM]
                q, k, v, gate, u = _ab_in(x, g[1], ab_w_in_b[i], cos, sin, s)
                mixed = _ab_mix(ab_decay[i], seq3(q), seq3(k), seq3(v), seq3(gate), ab_ret_norm_g[i][None],
                                seq3(u), ab_conv_w[i], ab_conv_b[i][None], ab_conv_norm_g[i][None])
                acts, w_pre = [flat(mixed)], ab_w_out_b[i]
            else:
                cos, sin = rope[DIFF_QK_DIM]
                reps = LANES // DIFF_QK_DIM
                qt, k, vt = _c_in(x, g[1], c_w_in_b[i], jnp.tile(c_q_norm_g[i], reps)[None],
                                  jnp.tile(c_k_norm_g[i], reps)[None], cos, sin, s)
                bound = (DIFF_QK_DIM * Q_SCALE_LOG2 * (1 + 2.0 ** -7) * jnp.max(jnp.abs(c_q_norm_g[i]))
                         * jnp.max(jnp.abs(c_k_norm_g[i])))
                o = _diff_attn(c_lambda[i], c_subln_g[i][None], qt, k, vt, _lambda_init(layer), bound)
                acts, w_pre = [flat(o)], c_w_out_b[i]
            x = _ffn(x, acts, w_pre, g[2], ffn_w_in_b, ffn_w_out_b, layer, 1)
        return x.reshape(b, s, d)

    return trunk(x_prompt), trunk(x_sample)
```

```python
import functools
import math

import jax
import jax.numpy as jnp
from jax import lax
from jax.experimental import pallas as pl
from jax.experimental.pallas import tpu as pltpu

F32 = jnp.float32
BF16 = jnp.bfloat16

EPS = 1e-6
ROPE_THETA = 10000.0
RET_HEADS = 4
RET_HEAD_DIM = 128
RET_DIM = RET_HEADS * RET_HEAD_DIM
RET_CHUNK = 128
CONV_DIM = 512
CONV_WIDTH = 31
CONV_PAD = CONV_WIDTH // 2
CONV_HALO = 16
DIFF_HEADS = 8
DIFF_QK_DIM = 64
DIFF_V_DIM = 128
DIFF_TOTAL = DIFF_HEADS * DIFF_V_DIM
LANES = 128
SUBLANES = 8

V7X_VMEM_BYTES = 64 * 1024 * 1024
VMEM_LIMIT = V7X_VMEM_BYTES - 8 * 1024 * 1024

FFN_TM = 1024
FFN_ROW_BLOCKS = 2
FFN_TF = 256
PROJ_TM = 512
RET_TS = 512
CONV_RB = 64
ATT_TQ = 512
ATT_TK = 1024
ATT_ONES_ROWS = 16
ATT_V_ROWS = DIFF_V_DIM + ATT_ONES_ROWS
ATT_SAFE_LOG2 = 50.0
Q_SCALE_LOG2 = DIFF_QK_DIM ** -0.5 * math.log2(math.e)


def _tile(n, pref):
    t = min(n, pref)
    while n % t:
        t //= 2
    return t


def _params(*sem):
    return pltpu.CompilerParams(dimension_semantics=sem, vmem_limit_bytes=VMEM_LIMIT)


def _resident(shape):
    zeros = (0,) * len(shape)
    return pl.BlockSpec(shape, lambda *_: zeros, pipeline_mode=pl.Buffered(1))


def _rms(x, g):
    return x * lax.rsqrt(jnp.mean(x * x, axis=-1, keepdims=True) + EPS) * g


def _silu(x):
    return x * jax.nn.sigmoid(x)


def _mm(a, b):
    return jnp.dot(a, b, preferred_element_type=F32)


def _ffn_kernel(*refs, n_acts, tf):
    x_ref, act_refs, rest = refs[0], refs[1:1 + n_acts], refs[1 + n_acts:]
    wp_ref = rest[0] if n_acts else None
    g_ref, wi_ref, wo_ref, o_ref, xn_ref, acc_ref = rest[1:] if n_acts else rest
    tm = x_ref.shape[0]
    blocks = [slice(r, r + tm // FFN_ROW_BLOCKS) for r in range(0, tm, tm // FFN_ROW_BLOCKS)]
    for rows in blocks:
        x = x_ref[rows, :]
        row = 0
        for a_ref in act_refs:
            x = x + _mm(a_ref[rows, :], wp_ref[row:row + a_ref.shape[1], :])
            row += a_ref.shape[1]
        o_ref[rows, :] = x
        xn_ref[rows, :] = _rms(x, g_ref[...]).astype(BF16)
    acc_ref[...] = jnp.zeros_like(acc_ref)
    ff = wo_ref.shape[0]
    for lo in range(0, ff, tf):
        for rows in blocks:
            xn = xn_ref[rows, :]
            h = _silu(_mm(xn, wi_ref[:, lo:lo + tf])) * _mm(xn, wi_ref[:, ff + lo:ff + lo + tf])
            acc_ref[rows, :] += _mm(h.astype(BF16), wo_ref[lo:lo + tf, :])
    for rows in blocks:
        o_ref[rows, :] += 0.5 * acc_ref[rows, :]


def _ffn(x, acts, w_pre, g, w_in, w_out, layer, half):
    t, d = x.shape
    ff = w_out.shape[2]
    tm = _tile(t, FFN_TM)
    tok = lambda i: (i, 0)
    pick = lambda i: (layer, half, 0, 0)
    return pl.pallas_call(
        functools.partial(_ffn_kernel, n_acts=len(acts), tf=_tile(ff, FFN_TF)),
        grid=(t // tm,),
        in_specs=[pl.BlockSpec((tm, d), tok)]
        + [pl.BlockSpec((tm, a.shape[1]), tok) for a in acts]
        + ([_resident(w_pre.shape)] if acts else [])
        + [
            _resident((1, d)),
            pl.BlockSpec((None, None, d, 2 * ff), pick, pipeline_mode=pl.Buffered(1)),
            pl.BlockSpec((None, None, ff, d), pick, pipeline_mode=pl.Buffered(1)),
        ],
        out_specs=pl.BlockSpec((tm, d), tok),
        out_shape=jax.ShapeDtypeStruct((t, d), F32),
        scratch_shapes=[pltpu.VMEM((tm, d), BF16), pltpu.VMEM((tm, d), F32)],
        compiler_params=_params("parallel"),
        name="ffn",
    )(x, *acts, *([w_pre] if acts else []), g, w_in, w_out)


def _ab_in_kernel(x_ref, g_ref, w_ref, cos_ref, sin_ref, q_ref, k_ref, v_ref, gate_ref, u_ref):
    xn = _rms(x_ref[...], g_ref[...]).astype(BF16)
    cos = cos_ref[...]
    sin = sin_ref[...]

    def proj(seg):
        return _mm(xn, w_ref[:, seg * RET_DIM:(seg + 1) * RET_DIM])

    def rope(y):
        return y * cos + pltpu.roll(y, RET_HEAD_DIM // 2, 1) * sin

    q = proj(0)
    k = proj(1)
    for h in range(RET_HEADS):
        cols = slice(h * RET_HEAD_DIM, (h + 1) * RET_HEAD_DIM)
        q_ref[:, cols] = rope(q[:, cols]).astype(BF16)
        k_ref[:, cols] = (rope(k[:, cols]) * RET_HEAD_DIM ** -0.5).astype(BF16)
    v_ref[...] = proj(2).astype(BF16)
    gate_ref[...] = proj(3)
    u_ref[...] = proj(4) * jax.nn.sigmoid(proj(5))


def _ab_in(x, g, w, cos, sin, seq):
    t, d = x.shape
    tm = _tile(seq, PROJ_TM)
    npos = seq // tm
    tok = lambda i: (i, 0)
    pos = lambda i: (i % npos, 0)
    wide = jax.ShapeDtypeStruct((t, RET_DIM), F32)
    narrow = jax.ShapeDtypeStruct((t, RET_DIM), BF16)
    return pl.pallas_call(
        _ab_in_kernel,
        grid=(t // tm,),
        in_specs=[
            pl.BlockSpec((tm, d), tok),
            _resident((1, d)),
            _resident(w.shape),
            pl.BlockSpec((tm, RET_HEAD_DIM), pos),
            pl.BlockSpec((tm, RET_HEAD_DIM), pos),
        ],
        out_specs=[pl.BlockSpec((tm, RET_DIM), tok)] * 5,
        out_shape=[narrow, narrow, narrow, wide, wide],
        compiler_params=_params("parallel"),
        name="ab_in",
    )(x, g, w, cos, sin)


def _ret_decays(decay_ref, direction, h):
    c = RET_CHUNK
    lg = -jnp.exp(jnp.full((1, 1), decay_ref[direction, h], F32))
    idx = lax.broadcasted_iota(jnp.int32, (c, 1), 0).astype(F32)
    if direction == 0:
        q_dec = jnp.exp(lg * (idx + 1.0))
        k_dec = jnp.exp(lg * (c - 1.0 - idx))
    else:
        q_dec = jnp.exp(lg * (c - idx))
        k_dec = jnp.exp(lg * idx)
    return lg, q_dec, k_dec, jnp.exp(lg * float(c))


def _ret_bwd_kernel(decay_ref, q_ref, k_ref, v_ref, o_ref, state_ref, *, ts):
    c = RET_CHUNK

    @pl.when(pl.program_id(1) == 0)
    def _():
        state_ref[...] = jnp.zeros_like(state_ref)

    dec = [_ret_decays(decay_ref, 1, h) for h in range(RET_HEADS)]
    for ci in reversed(range(ts // c)):
        rows = slice(ci * c, (ci + 1) * c)
        for h in range(RET_HEADS):
            cols = slice(h * RET_HEAD_DIM, (h + 1) * RET_HEAD_DIM)
            _, q_dec, k_dec, c_dec = dec[h]
            qc = q_ref[0, rows, cols].astype(F32)
            kc = k_ref[0, rows, cols].astype(F32)
            vc = v_ref[0, rows, cols]
            state = state_ref[h]
            o_ref[0, rows, cols] = _mm((qc * q_dec).astype(BF16), state.astype(BF16))
            state_ref[h] = state * c_dec + _mm((kc * k_dec).T.astype(BF16), vc)


def _ret_fwd_kernel(decay_ref, q_ref, k_ref, v_ref, ob_ref, gate_ref, gn_ref, o_ref, state_ref, *, ts):
    c = RET_CHUNK

    @pl.when(pl.program_id(1) == 0)
    def _():
        state_ref[...] = jnp.zeros_like(state_ref)

    diff = (lax.broadcasted_iota(jnp.int32, (c, c), 0) - lax.broadcasted_iota(jnp.int32, (c, c), 1)).astype(F32)
    dec = []
    for h in range(RET_HEADS):
        lgf, q_dec, k_dec, c_dec = _ret_decays(decay_ref, 0, h)
        lgb = _ret_decays(decay_ref, 1, h)[0]
        dmat = jnp.where(diff >= 0, jnp.exp(lgf * jnp.maximum(diff, 0.0)), jnp.exp(lgb * jnp.maximum(-diff, 0.0)))
        dec.append((dmat, q_dec, k_dec, c_dec))
    for ci in range(ts // c):
        rows = slice(ci * c, (ci + 1) * c)
        for h in range(RET_HEADS):
            cols = slice(h * RET_HEAD_DIM, (h + 1) * RET_HEAD_DIM)
            dmat, q_dec, k_dec, c_dec = dec[h]
            qb = q_ref[0, rows, cols]
            kb = k_ref[0, rows, cols]
            qc = qb.astype(F32)
            kc = kb.astype(F32)
            vc = v_ref[0, rows, cols]
            state = state_ref[h]
            scores = lax.dot_general(qb, kb, (((1,), (1,)), ((), ())), preferred_element_type=F32) * dmat
            o = _mm(scores.astype(BF16), vc) + _mm((qc * q_dec).astype(BF16), state.astype(BF16))
            o = o + ob_ref[0, rows, cols]
            state_ref[h] = state * c_dec + _mm((kc * k_dec).T.astype(BF16), vc)
            o_ref[0, rows, cols] = (_silu(gate_ref[0, rows, cols]) * _rms(o, gn_ref[:, cols])).astype(BF16)


def _ret_conv_kernel(decay_ref, q_ref, k_ref, v_ref, ob_ref, gate_ref, gn_ref, u_ref, prev_ref, next_ref,
                     w_ref, b_ref, g_ref, o_ref, state_ref, buf_ref, *, ts, rb):
    _ret_fwd_kernel(decay_ref, q_ref, k_ref, v_ref, ob_ref, gate_ref, gn_ref, o_ref, state_ref, ts=ts)
    _conv_kernel(u_ref, prev_ref, next_ref, w_ref, b_ref, g_ref, o_ref, buf_ref, ts=ts, rb=rb)


def _ab_mix(decay, q, k, v, gate, gn, u, w, bias, g):
    b, s, d = q.shape
    ts = _tile(s, RET_TS)
    nt = s // ts
    per = ts // CONV_HALO
    nh = s // CONV_HALO
    state = pltpu.VMEM((RET_HEADS, RET_HEAD_DIM, RET_HEAD_DIM), F32)
    fwd = lambda bi, ti: (bi, ti, 0)
    rev = lambda bi, ti: (bi, nt - 1 - ti, 0)
    smem = pl.BlockSpec(memory_space=pltpu.SMEM)
    ob = pl.pallas_call(
        functools.partial(_ret_bwd_kernel, ts=ts),
        grid=(b, nt),
        in_specs=[smem] + [pl.BlockSpec((1, ts, d), rev)] * 3,
        out_specs=pl.BlockSpec((1, ts, d), rev),
        out_shape=jax.ShapeDtypeStruct((b, s, d), F32),
        scratch_shapes=[state],
        compiler_params=_params("parallel", "arbitrary"),
        name="ret_bwd",
    )(decay, q, k, v)
    return pl.pallas_call(
        functools.partial(_ret_conv_kernel, ts=ts, rb=_tile(ts, CONV_RB)),
        grid=(b, nt),
        in_specs=[smem] + [pl.BlockSpec((1, ts, d), fwd)] * 5 + [_resident(gn.shape)] + [
            pl.BlockSpec((1, ts, CONV_DIM), fwd),
            pl.BlockSpec((1, CONV_HALO, CONV_DIM), lambda bi, ti: (bi, jnp.maximum(ti * per - 1, 0), 0)),
            pl.BlockSpec((1, CONV_HALO, CONV_DIM), lambda bi, ti: (bi, jnp.minimum((ti + 1) * per, nh - 1), 0)),
            _resident(w.shape),
            _resident(bias.shape),
            _resident(g.shape),
        ],
        out_specs=pl.BlockSpec((1, ts, d + CONV_DIM), fwd),
        out_shape=jax.ShapeDtypeStruct((b, s, d + CONV_DIM), BF16),
        scratch_shapes=[state, pltpu.VMEM((SUBLANES, ts + 2 * CONV_HALO, CONV_DIM), F32)],
        compiler_params=_params("parallel", "arbitrary"),
        name="ret_conv",
    )(decay, q, k, v, ob, gate, gn, u, u, u, w, bias, g)


def _conv_kernel(u_ref, prev_ref, next_ref, w_ref, b_ref, g_ref, o_ref, buf_ref, *, ts, rb):
    i = pl.program_id(1)
    zero = jnp.zeros((CONV_HALO, CONV_DIM), F32)
    buf_ref[0, 0:CONV_HALO, :] = jnp.where(i > 0, prev_ref[0], zero)
    buf_ref[0, CONV_HALO:CONV_HALO + ts, :] = u_ref[0]
    buf_ref[0, CONV_HALO + ts:, :] = jnp.where(i < pl.num_programs(1) - 1, next_ref[0], zero)
    rows = ts + 2 * CONV_HALO - SUBLANES
    for r in range(1, SUBLANES):
        buf_ref[r, 0:rows, :] = buf_ref[0, r:r + rows, :]
    first = CONV_HALO - CONV_PAD

    w = w_ref[...]
    for r0 in range(0, ts, rb):
        acc = jnp.zeros((rb, CONV_DIM), F32)
        for t in range(CONV_WIDTH):
            shift, base = (r0 + first + t) % SUBLANES, (r0 + first + t) // SUBLANES * SUBLANES
            acc = acc + w[t:t + 1, :] * buf_ref[shift, base:base + rb, :]
        y = _rms(acc + b_ref[...], g_ref[...])
        o_ref[0, r0:r0 + rb, RET_DIM:] = _silu(y).astype(BF16)


def _c_in_kernel(x_ref, g_ref, w_ref, qg_ref, kg_ref, cos_ref, sin_ref, qt_ref, k_ref, vt_ref):
    xn = _rms(x_ref[...], g_ref[...]).astype(BF16)
    cos = cos_ref[...]
    sin = sin_ref[...]
    row = lax.broadcasted_iota(jnp.int32, (LANES, LANES), 0) >= DIFF_QK_DIM
    col = lax.broadcasted_iota(jnp.int32, (LANES, LANES), 1) >= DIFF_QK_DIM
    group = jnp.where(row == col, 1.0, 0.0).astype(BF16)
    group = jnp.concatenate([group, group], axis=0)
    lane = lax.broadcasted_iota(jnp.int32, (1, LANES), 1)
    low_half = jnp.bitwise_and(lane, DIFF_QK_DIM // 2) == 0
    wide = 2 * LANES

    def head_pair(y, gain, scale):
        sq = y * y
        hi = sq.astype(BF16)
        lo = (sq - hi.astype(F32)).astype(BF16)
        ms = _mm(jnp.concatenate([hi, lo], axis=1), group) * (1.0 / DIFF_QK_DIM)
        yn = y * lax.rsqrt(ms + EPS) * gain
        half = DIFF_QK_DIM // 2
        partner = jnp.where(low_half, pltpu.roll(yn, LANES - half, 1), pltpu.roll(yn, half, 1))
        return (yn * cos + partner * sin) * scale

    ones = jnp.ones((ATT_ONES_ROWS, x_ref.shape[0]), BF16)
    for cb in range(0, DIFF_TOTAL, wide):
        yq = _mm(xn, w_ref[:, cb:cb + wide])
        yk = _mm(xn, w_ref[:, DIFF_TOTAL + cb:DIFF_TOTAL + cb + wide])
        yv = _mm(xn, w_ref[:, 2 * DIFF_TOTAL + cb:2 * DIFF_TOTAL + cb + wide])
        for sub in range(0, wide, LANES):
            cols = slice(sub, sub + LANES)
            head = (cb + sub) // DIFF_V_DIM
            q = head_pair(yq[:, cols], qg_ref[...], Q_SCALE_LOG2)
            qt_ref[0, cb + sub:cb + sub + LANES, :] = q.T.astype(BF16)
            k_ref[0, head] = head_pair(yk[:, cols], kg_ref[...], 1.0).astype(BF16)
            base = head * ATT_V_ROWS
            vt_ref[0, 0, base:base + DIFF_V_DIM, :] = yv[:, cols].T.astype(BF16)
            vt_ref[0, 0, base + DIFF_V_DIM:base + ATT_V_ROWS, :] = ones


def _c_in(x, g, w, qg, kg, cos, sin, seq):
    t, d = x.shape
    tm = _tile(seq, ATT_TK)
    npos = seq // tm
    b = t // seq
    tok = lambda i: (i, 0)
    pos = lambda i: (i % npos, 0)
    vrows = DIFF_HEADS * ATT_V_ROWS
    return pl.pallas_call(
        _c_in_kernel,
        grid=(t // tm,),
        in_specs=[
            pl.BlockSpec((tm, d), tok),
            _resident((1, d)),
            _resident(w.shape),
            _resident(qg.shape),
            _resident(kg.shape),
            pl.BlockSpec((tm, LANES), pos),
            pl.BlockSpec((tm, LANES), pos),
        ],
        out_specs=[
            pl.BlockSpec((1, DIFF_TOTAL, tm), lambda i: (i // npos, 0, i % npos)),
            pl.BlockSpec((1, DIFF_HEADS, tm, DIFF_V_DIM), lambda i: (i // npos, 0, i % npos, 0)),
            pl.BlockSpec((1, 1, vrows, tm), lambda i: (i // npos, i % npos, 0, 0)),
        ],
        out_shape=[
            jax.ShapeDtypeStruct((b, DIFF_TOTAL, seq), BF16),
            jax.ShapeDtypeStruct((b, DIFF_HEADS, seq, DIFF_V_DIM), BF16),
            jax.ShapeDtypeStruct((b, npos, vrows, tm), BF16),
        ],
        compiler_params=_params("parallel"),
        name="c_in",
    )(x, g, w, qg, kg, cos, sin)


def _attn_queries(qt_ref):
    qt = qt_ref[0].astype(F32)
    row = lax.broadcasted_iota(jnp.int32, qt.shape, 0)
    return jnp.concatenate([jnp.where(row < DIFF_QK_DIM, qt, 0.0), jnp.where(row >= DIFF_QK_DIM, qt, 0.0)],
                           axis=1).astype(BF16)


def _attn_finish(lam_ref, sg_ref, num, den, o_ref, *, tq, lam_init):
    lam = lam_ref[...]
    lam_full = (jnp.exp(jnp.sum(lam[0:1] * lam[1:2], axis=1, keepdims=True))
                - jnp.exp(jnp.sum(lam[2:3] * lam[3:4], axis=1, keepdims=True)) + lam_init)
    out = num / den
    o = out[:, :tq] - lam_full * out[:, tq:]
    o = o * lax.rsqrt(jnp.mean(o * o, axis=0, keepdims=True) + EPS)
    o_ref[0] = (o.T * sg_ref[...] * (1.0 - lam_init)).astype(BF16)


def _attn_bounded_kernel(lam_ref, sg_ref, qt_ref, k_ref, vt_ref, o_ref, acc_ref, *, tq, tk, nk, lam_init):
    qs = _attn_queries(qt_ref)
    acc_ref[...] = jnp.zeros_like(acc_ref)

    def body(j, carry):
        kj = k_ref[0, 0, pl.ds(pl.multiple_of(j * tk, tk), tk), :]
        acc_ref[...] += _mm(vt_ref[0, j], jnp.exp2(_mm(kj, qs)).astype(BF16))
        return carry

    lax.fori_loop(0, nk, body, 0, unroll=16)
    acc = acc_ref[...]
    _attn_finish(lam_ref, sg_ref, acc[:DIFF_V_DIM], acc[DIFF_V_DIM:DIFF_V_DIM + 1], o_ref, tq=tq, lam_init=lam_init)


def _attn_online_kernel(lam_ref, sg_ref, qt_ref, k_ref, vt_ref, o_ref, acc_ref, s_ref, *, tq, tk, nk, lam_init):
    qs = _attn_queries(qt_ref)
    acc_ref[...] = jnp.zeros_like(acc_ref)

    def scores(j, slot):
        s = _mm(k_ref[0, 0, pl.ds(pl.multiple_of(j * tk, tk), tk), :], qs)
        s_ref[slot] = s
        return jnp.max(s, axis=0, keepdims=True)

    def step(j, slot, carry, prefetch=True):
        m_prev, tile_max = carry
        next_max = scores(j + 1, 1 - slot) if prefetch else tile_max
        m_new = jnp.maximum(m_prev, tile_max)
        p = jnp.exp2(s_ref[slot] - m_new).astype(BF16)
        acc_ref[...] = jnp.exp2(m_prev - m_new) * acc_ref[...] + _mm(vt_ref[0, j], p)
        return m_new, next_max

    def pair(jj, carry):
        return step(2 * jj + 1, 1, step(2 * jj, 0, carry))

    carry = (jnp.full((1, 2 * tq), -jnp.inf, F32), scores(0, 0))
    carry = lax.fori_loop(0, nk // 2 - 1, pair, carry)
    step(nk - 1, 1, step(nk - 2, 0, carry), prefetch=False)
    acc = acc_ref[...]
    _attn_finish(lam_ref, sg_ref, acc[:DIFF_V_DIM], acc[DIFF_V_DIM:DIFF_V_DIM + 1], o_ref, tq=tq, lam_init=lam_init)


def _diff_attn(lam, sg, qt, k, vt, lam_init, score_bound):
    b, _, s, _ = k.shape
    d = DIFF_TOTAL
    nk, tk = vt.shape[1], vt.shape[3]
    assert nk % 2 == 0, "key tiles are consumed in pairs"
    tq = _tile(s, ATT_TQ)

    def call(body, name, scratch):
        return pl.pallas_call(
            functools.partial(body, tq=tq, tk=tk, nk=nk, lam_init=lam_init),
            grid=(b, DIFF_HEADS, s // tq),
            in_specs=[
                _resident(lam.shape),
                _resident(sg.shape),
                pl.BlockSpec((1, DIFF_V_DIM, tq), lambda bi, h, qi: (bi, h, qi)),
                pl.BlockSpec((1, 1, s, DIFF_V_DIM), lambda bi, h, qi: (bi, h, 0, 0)),
                pl.BlockSpec((1, nk, ATT_V_ROWS, tk), lambda bi, h, qi: (bi, 0, h, 0)),
            ],
            out_specs=pl.BlockSpec((1, tq, DIFF_V_DIM), lambda bi, h, qi: (bi, qi, h)),
            out_shape=jax.ShapeDtypeStruct((b, s, d), BF16),
            scratch_shapes=scratch,
            compiler_params=_params("parallel", "parallel", "arbitrary"),
            name=name,
        )

    acc = pltpu.VMEM((ATT_V_ROWS, 2 * tq), F32)
    bounded = call(_attn_bounded_kernel, "diff_attn_bounded", [acc])
    online = call(_attn_online_kernel, "diff_attn_online", [acc, pltpu.VMEM((2, tk, 2 * tq), F32)])
    return lax.cond(score_bound <= ATT_SAFE_LOG2, bounded, online, lam, sg, qt, k, vt)


def _rope_tables(seq, head_dim):
    half = head_dim // 2
    inv = ROPE_THETA ** (-jnp.arange(half, dtype=F32) / half)
    ang = jnp.arange(seq, dtype=F32)[:, None] * inv[None, :]
    cos = jnp.cos(ang)
    sin = jnp.sin(ang)
    reps = LANES // head_dim
    return (jnp.tile(jnp.concatenate([cos, cos], axis=-1), (1, reps)),
            jnp.tile(jnp.concatenate([-sin, sin], axis=-1), (1, reps)))


def _lambda_init(layer):
    return 0.8 - 0.6 * math.exp(-0.3 * layer)


def kernel(x_prompt, x_sample, norm_g, ffn_w_in, ffn_w_out, ab_w_in, ab_decay, ab_ret_norm_g, ab_conv_w, ab_conv_b,
           ab_conv_norm_g, ab_w_out, c_w_in, c_q_norm_g, c_k_norm_g, c_lambda, c_subln_g, c_w_out):
    depth = norm_g.shape[0]
    d = norm_g.shape[-1]
    ffn_w_in_b = ffn_w_in.astype(BF16)
    ffn_w_out_b = ffn_w_out.astype(BF16)
    ab_w_in_b = ab_w_in.astype(BF16)
    ab_w_out_b = ab_w_out.astype(BF16)
    c_w_in_b = c_w_in.astype(BF16)
    c_w_out_b = c_w_out.astype(BF16)
    max_seq = max(x_prompt.shape[1], x_sample.shape[1])
    rope = {dim: _rope_tables(max_seq, dim) for dim in (RET_HEAD_DIM, DIFF_QK_DIM)}

    def trunk(x3):
        b, s, _ = x3.shape
        x = x3.reshape(b * s, d)
        seq3 = lambda a: a.reshape(b, s, a.shape[-1])
        flat = lambda a: a.reshape(b * s, a.shape[-1])
        for layer in range(depth):
            g = norm_g[layer][:, None, :]
            x = _ffn(x, [], None, g[0], ffn_w_in_b, ffn_w_out_b, layer, 0)
            i = layer // 2
            if layer % 2 == 0:
                cos, sin = rope[RET_HEAD_DIM]
                q, k, v, gate, u = _ab_in(x, g[1], ab_w_in_b[i], cos, sin, s)
                mixed = _ab_mix(ab_decay[i], seq3(q), seq3(k), seq3(v), seq3(gate), ab_ret_norm_g[i][None],
                                seq3(u), ab_conv_w[i], ab_conv_b[i][None], ab_conv_norm_g[i][None])
                acts, w_pre = [flat(mixed)], ab_w_out_b[i]
            else:
                cos, sin = rope[DIFF_QK_DIM]
                reps = LANES // DIFF_QK_DIM
                qt, k, vt = _c_in(x, g[1], c_w_in_b[i], jnp.tile(c_q_norm_g[i], reps)[None],
                                  jnp.tile(c_k_norm_g[i], reps)[None], cos, sin, s)
                bound = (DIFF_QK_DIM * Q_SCALE_LOG2 * (1 + 2.0 ** -7) * jnp.max(jnp.abs(c_q_norm_g[i]))
                         * jnp.max(jnp.abs(c_k_norm_g[i])))
                o = _diff_attn(c_lambda[i], c_subln_g[i][None], qt, k, vt, _lambda_init(layer), bound)
                acts, w_pre = [flat(o)], c_w_out_b[i]
            x = _ffn(x, acts, w_pre, g[2], ffn_w_in_b, ffn_w_out_b, layer, 1)
        return x.reshape(b, s, d)

    return trunk(x_prompt), trunk(x_sample)
```

```python
import functools
import math

import jax
import jax.numpy as jnp
from jax import lax
from jax.experimental import pallas as pl
from jax.experimental.pallas import tpu as pltpu

F32 = jnp.float32
BF16 = jnp.bfloat16

EPS = 1e-6
ROPE_THETA = 10000.0
RET_HEADS = 4
RET_HEAD_DIM = 128
RET_DIM = RET_HEADS * RET_HEAD_DIM
RET_CHUNK = 128
CONV_DIM = 512
CONV_WIDTH = 31
CONV_PAD = CONV_WIDTH // 2
CONV_HALO = 16
DIFF_HEADS = 8
DIFF_QK_DIM = 64
DIFF_V_DIM = 128
DIFF_TOTAL = DIFF_HEADS * DIFF_V_DIM
LANES = 128
SUBLANES = 8

V7X_VMEM_BYTES = 64 * 1024 * 1024
VMEM_LIMIT = V7X_VMEM_BYTES - 8 * 1024 * 1024

FFN_TM = 1024
FFN_ROW_BLOCKS = 2
FFN_TF = 256
PROJ_TM = 512
RET_TS = 512
CONV_RB = 64
ATT_TQ = 512
ATT_TK = 2048
C_IN_TM = 1024
ATT_ONES_ROWS = 16
ATT_V_ROWS = DIFF_V_DIM + ATT_ONES_ROWS
ATT_SAFE_LOG2 = 50.0
Q_SCALE_LOG2 = DIFF_QK_DIM ** -0.5 * math.log2(math.e)


def _tile(n, pref):
    t = min(n, pref)
    while n % t:
        t //= 2
    return t


def _params(*sem):
    return pltpu.CompilerParams(dimension_semantics=sem, vmem_limit_bytes=VMEM_LIMIT)


def _resident(shape):
    zeros = (0,) * len(shape)
    return pl.BlockSpec(shape, lambda *_: zeros, pipeline_mode=pl.Buffered(1))


def _rms(x, g):
    return x * lax.rsqrt(jnp.mean(x * x, axis=-1, keepdims=True) + EPS) * g


def _silu(x):
    return x * jax.nn.sigmoid(x)


def _mm(a, b):
    return jnp.dot(a, b, preferred_element_type=F32)


def _ffn_kernel(*refs, n_acts, tf):
    x_ref, act_refs, rest = refs[0], refs[1:1 + n_acts], refs[1 + n_acts:]
    wp_ref = rest[0] if n_acts else None
    g_ref, wi_ref, wo_ref, o_ref, xn_ref, acc_ref = rest[1:] if n_acts else rest
    tm = x_ref.shape[0]
    blocks = [slice(r, r + tm // FFN_ROW_BLOCKS) for r in range(0, tm, tm // FFN_ROW_BLOCKS)]
    for rows in blocks:
        x = x_ref[rows, :]
        row = 0
        for a_ref in act_refs:
            x = x + _mm(a_ref[rows, :], wp_ref[row:row + a_ref.shape[1], :])
            row += a_ref.shape[1]
        o_ref[rows, :] = x
        xn_ref[rows, :] = _rms(x, g_ref[...]).astype(BF16)
    acc_ref[...] = jnp.zeros_like(acc_ref)
    ff = wo_ref.shape[0]
    for lo in range(0, ff, tf):
        for rows in blocks:
            xn = xn_ref[rows, :]
            h = _silu(_mm(xn, wi_ref[:, lo:lo + tf])) * _mm(xn, wi_ref[:, ff + lo:ff + lo + tf])
            acc_ref[rows, :] += _mm(h.astype(BF16), wo_ref[lo:lo + tf, :])
    for rows in blocks:
        o_ref[rows, :] += 0.5 * acc_ref[rows, :]


def _ffn(x, acts, w_pre, g, w_in, w_out, layer, half):
    t, d = x.shape
    ff = w_out.shape[2]
    tm = _tile(t, FFN_TM)
    tok = lambda i: (i, 0)
    pick = lambda i: (layer, half, 0, 0)
    return pl.pallas_call(
        functools.partial(_ffn_kernel, n_acts=len(acts), tf=_tile(ff, FFN_TF)),
        grid=(t // tm,),
        in_specs=[pl.BlockSpec((tm, d), tok)]
        + [pl.BlockSpec((tm, a.shape[1]), tok) for a in acts]
        + ([_resident(w_pre.shape)] if acts else [])
        + [
            _resident((1, d)),
            pl.BlockSpec((None, None, d, 2 * ff), pick, pipeline_mode=pl.Buffered(1)),
            pl.BlockSpec((None, None, ff, d), pick, pipeline_mode=pl.Buffered(1)),
        ],
        out_specs=pl.BlockSpec((tm, d), tok),
        out_shape=jax.ShapeDtypeStruct((t, d), F32),
        scratch_shapes=[pltpu.VMEM((tm, d), BF16), pltpu.VMEM((tm, d), F32)],
        compiler_params=_params("parallel"),
        name="ffn",
    )(x, *acts, *([w_pre] if acts else []), g, w_in, w_out)


def _ab_in_kernel(x_ref, g_ref, w_ref, cos_ref, sin_ref, q_ref, k_ref, v_ref, gate_ref, u_ref):
    xn = _rms(x_ref[...], g_ref[...]).astype(BF16)
    cos = cos_ref[...]
    sin = sin_ref[...]

    def proj(seg):
        return _mm(xn, w_ref[:, seg * RET_DIM:(seg + 1) * RET_DIM])

    def rope(y):
        return y * cos + pltpu.roll(y, RET_HEAD_DIM // 2, 1) * sin

    q = proj(0)
    k = proj(1)
    for h in range(RET_HEADS):
        cols = slice(h * RET_HEAD_DIM, (h + 1) * RET_HEAD_DIM)
        q_ref[:, cols] = rope(q[:, cols]).astype(BF16)
        k_ref[:, cols] = (rope(k[:, cols]) * RET_HEAD_DIM ** -0.5).astype(BF16)
    v_ref[...] = proj(2).astype(BF16)
    gate_ref[...] = proj(3)
    u_ref[...] = proj(4) * jax.nn.sigmoid(proj(5))


def _ab_in(x, g, w, cos, sin, seq):
    t, d = x.shape
    tm = _tile(seq, PROJ_TM)
    npos = seq // tm
    tok = lambda i: (i, 0)
    pos = lambda i: (i % npos, 0)
    wide = jax.ShapeDtypeStruct((t, RET_DIM), F32)
    narrow = jax.ShapeDtypeStruct((t, RET_DIM), BF16)
    return pl.pallas_call(
        _ab_in_kernel,
        grid=(t // tm,),
        in_specs=[
            pl.BlockSpec((tm, d), tok),
            _resident((1, d)),
            _resident(w.shape),
            pl.BlockSpec((tm, RET_HEAD_DIM), pos),
            pl.BlockSpec((tm, RET_HEAD_DIM), pos),
        ],
        out_specs=[pl.BlockSpec((tm, RET_DIM), tok)] * 5,
        out_shape=[narrow, narrow, narrow, wide, wide],
        compiler_params=_params("parallel"),
        name="ab_in",
    )(x, g, w, cos, sin)


def _ret_decays(decay_ref, direction, h):
    c = RET_CHUNK
    lg = -jnp.exp(jnp.full((1, 1), decay_ref[direction, h], F32))
    idx = lax.broadcasted_iota(jnp.int32, (c, 1), 0).astype(F32)
    if direction == 0:
        q_dec = jnp.exp(lg * (idx + 1.0))
        k_dec = jnp.exp(lg * (c - 1.0 - idx))
    else:
        q_dec = jnp.exp(lg * (c - idx))
        k_dec = jnp.exp(lg * idx)
    return lg, q_dec, k_dec, jnp.exp(lg * float(c))


def _ret_bwd_kernel(decay_ref, q_ref, k_ref, v_ref, o_ref, state_ref, *, ts):
    c = RET_CHUNK

    @pl.when(pl.program_id(1) == 0)
    def _():
        state_ref[...] = jnp.zeros_like(state_ref)

    dec = [_ret_decays(decay_ref, 1, h) for h in range(RET_HEADS)]
    for ci in reversed(range(ts // c)):
        rows = slice(ci * c, (ci + 1) * c)
        for h in range(RET_HEADS):
            cols = slice(h * RET_HEAD_DIM, (h + 1) * RET_HEAD_DIM)
            _, q_dec, k_dec, c_dec = dec[h]
            qc = q_ref[0, rows, cols].astype(F32)
            kc = k_ref[0, rows, cols].astype(F32)
            vc = v_ref[0, rows, cols]
            state = state_ref[h]
            o_ref[0, rows, cols] = _mm((qc * q_dec).astype(BF16), state.astype(BF16))
            state_ref[h] = state * c_dec + _mm((kc * k_dec).T.astype(BF16), vc)


def _ret_fwd_kernel(decay_ref, q_ref, k_ref, v_ref, ob_ref, gate_ref, gn_ref, o_ref, state_ref, *, ts):
    c = RET_CHUNK

    @pl.when(pl.program_id(1) == 0)
    def _():
        state_ref[...] = jnp.zeros_like(state_ref)

    diff = (lax.broadcasted_iota(jnp.int32, (c, c), 0) - lax.broadcasted_iota(jnp.int32, (c, c), 1)).astype(F32)
    dec = []
    for h in range(RET_HEADS):
        lgf, q_dec, k_dec, c_dec = _ret_decays(decay_ref, 0, h)
        lgb = _ret_decays(decay_ref, 1, h)[0]
        dmat = jnp.where(diff >= 0, jnp.exp(lgf * jnp.maximum(diff, 0.0)), jnp.exp(lgb * jnp.maximum(-diff, 0.0)))
        dec.append((dmat, q_dec, k_dec, c_dec))
    for ci in range(ts // c):
        rows = slice(ci * c, (ci + 1) * c)
        for h in range(RET_HEADS):
            cols = slice(h * RET_HEAD_DIM, (h + 1) * RET_HEAD_DIM)
            dmat, q_dec, k_dec, c_dec = dec[h]
            qb = q_ref[0, rows, cols]
            kb = k_ref[0, rows, cols]
            qc = qb.astype(F32)
            kc = kb.astype(F32)
            vc = v_ref[0, rows, cols]
            state = state_ref[h]
            scores = lax.dot_general(qb, kb, (((1,), (1,)), ((), ())), preferred_element_type=F32) * dmat
            o = _mm(scores.astype(BF16), vc) + _mm((qc * q_dec).astype(BF16), state.astype(BF16))
            o = o + ob_ref[0, rows, cols]
            state_ref[h] = state * c_dec + _mm((kc * k_dec).T.astype(BF16), vc)
            o_ref[0, rows, cols] = (_silu(gate_ref[0, rows, cols]) * _rms(o, gn_ref[:, cols])).astype(BF16)


def _ret_conv_kernel(decay_ref, q_ref, k_ref, v_ref, ob_ref, gate_ref, gn_ref, u_ref, prev_ref, next_ref,
                     w_ref, b_ref, g_ref, o_ref, state_ref, buf_ref, *, ts, rb):
    _ret_fwd_kernel(decay_ref, q_ref, k_ref, v_ref, ob_ref, gate_ref, gn_ref, o_ref, state_ref, ts=ts)
    _conv_kernel(u_ref, prev_ref, next_ref, w_ref, b_ref, g_ref, o_ref, buf_ref, ts=ts, rb=rb)


def _ab_mix(decay, q, k, v, gate, gn, u, w, bias, g):
    b, s, d = q.shape
    ts = _tile(s, RET_TS)
    nt = s // ts
    per = ts // CONV_HALO
    nh = s // CONV_HALO
    state = pltpu.VMEM((RET_HEADS, RET_HEAD_DIM, RET_HEAD_DIM), F32)
    fwd = lambda bi, ti: (bi, ti, 0)
    rev = lambda bi, ti: (bi, nt - 1 - ti, 0)
    smem = pl.BlockSpec(memory_space=pltpu.SMEM)
    ob = pl.pallas_call(
        functools.partial(_ret_bwd_kernel, ts=ts),
        grid=(b, nt),
        in_specs=[smem] + [pl.BlockSpec((1, ts, d), rev)] * 3,
        out_specs=pl.BlockSpec((1, ts, d), rev),
        out_shape=jax.ShapeDtypeStruct((b, s, d), F32),
        scratch_shapes=[state],
        compiler_params=_params("parallel", "arbitrary"),
        name="ret_bwd",
    )(decay, q, k, v)
    return pl.pallas_call(
        functools.partial(_ret_conv_kernel, ts=ts, rb=_tile(ts, CONV_RB)),
        grid=(b, nt),
        in_specs=[smem] + [pl.BlockSpec((1, ts, d), fwd)] * 5 + [_resident(gn.shape)] + [
            pl.BlockSpec((1, ts, CONV_DIM), fwd),
            pl.BlockSpec((1, CONV_HALO, CONV_DIM), lambda bi, ti: (bi, jnp.maximum(ti * per - 1, 0), 0)),
            pl.BlockSpec((1, CONV_HALO, CONV_DIM), lambda bi, ti: (bi, jnp.minimum((ti + 1) * per, nh - 1), 0)),
            _resident(w.shape),
            _resident(bias.shape),
            _resident(g.shape),
        ],
        out_specs=pl.BlockSpec((1, ts, d + CONV_DIM), fwd),
        out_shape=jax.ShapeDtypeStruct((b, s, d + CONV_DIM), BF16),
        scratch_shapes=[state, pltpu.VMEM((SUBLANES, ts + 2 * CONV_HALO, CONV_DIM), F32)],
        compiler_params=_params("parallel", "arbitrary"),
        name="ret_conv",
    )(decay, q, k, v, ob, gate, gn, u, u, u, w, bias, g)


def _conv_kernel(u_ref, prev_ref, next_ref, w_ref, b_ref, g_ref, o_ref, buf_ref, *, ts, rb):
    i = pl.program_id(1)
    zero = jnp.zeros((CONV_HALO, CONV_DIM), F32)
    buf_ref[0, 0:CONV_HALO, :] = jnp.where(i > 0, prev_ref[0], zero)
    buf_ref[0, CONV_HALO:CONV_HALO + ts, :] = u_ref[0]
    buf_ref[0, CONV_HALO + ts:, :] = jnp.where(i < pl.num_programs(1) - 1, next_ref[0], zero)
    rows = ts + 2 * CONV_HALO - SUBLANES
    for r in range(1, SUBLANES):
        buf_ref[r, 0:rows, :] = buf_ref[0, r:r + rows, :]
    first = CONV_HALO - CONV_PAD

    w = w_ref[...]
    for r0 in range(0, ts, rb):
        acc = jnp.zeros((rb, CONV_DIM), F32)
        for t in range(CONV_WIDTH):
            shift, base = (r0 + first + t) % SUBLANES, (r0 + first + t) // SUBLANES * SUBLANES
            acc = acc + w[t:t + 1, :] * buf_ref[shift, base:base + rb, :]
        y = _rms(acc + b_ref[...], g_ref[...])
        o_ref[0, r0:r0 + rb, RET_DIM:] = _silu(y).astype(BF16)


def _c_in_kernel(x_ref, g_ref, w_ref, qg_ref, kg_ref, cos_ref, sin_ref, qt_ref, k_ref, vt_ref):
    xn = _rms(x_ref[...], g_ref[...]).astype(BF16)
    cos = cos_ref[...]
    sin = sin_ref[...]
    row = lax.broadcasted_iota(jnp.int32, (LANES, LANES), 0) >= DIFF_QK_DIM
    col = lax.broadcasted_iota(jnp.int32, (LANES, LANES), 1) >= DIFF_QK_DIM
    group = jnp.where(row == col, 1.0, 0.0).astype(BF16)
    group = jnp.concatenate([group, group], axis=0)
    lane = lax.broadcasted_iota(jnp.int32, (1, LANES), 1)
    low_half = jnp.bitwise_and(lane, DIFF_QK_DIM // 2) == 0
    wide = 2 * LANES

    def head_pair(y, gain, scale):
        sq = y * y
        hi = sq.astype(BF16)
        lo = (sq - hi.astype(F32)).astype(BF16)
        ms = _mm(jnp.concatenate([hi, lo], axis=1), group) * (1.0 / DIFF_QK_DIM)
        yn = y * lax.rsqrt(ms + EPS) * gain
        half = DIFF_QK_DIM // 2
        partner = jnp.where(low_half, pltpu.roll(yn, LANES - half, 1), pltpu.roll(yn, half, 1))
        return (yn * cos + partner * sin) * scale

    ones = jnp.ones((ATT_ONES_ROWS, x_ref.shape[0]), BF16)
    for cb in range(0, DIFF_TOTAL, wide):
        yq = _mm(xn, w_ref[:, cb:cb + wide])
        yk = _mm(xn, w_ref[:, DIFF_TOTAL + cb:DIFF_TOTAL + cb + wide])
        yv = _mm(xn, w_ref[:, 2 * DIFF_TOTAL + cb:2 * DIFF_TOTAL + cb + wide])
        for sub in range(0, wide, LANES):
            cols = slice(sub, sub + LANES)
            head = (cb + sub) // DIFF_V_DIM
            q = head_pair(yq[:, cols], qg_ref[...], Q_SCALE_LOG2)
            qt_ref[0, cb + sub:cb + sub + LANES, :] = q.T.astype(BF16)
            k_ref[0, head] = head_pair(yk[:, cols], kg_ref[...], 1.0).astype(BF16)
            base = head * ATT_V_ROWS
            vt_ref[0, 0, base:base + DIFF_V_DIM, :] = yv[:, cols].T.astype(BF16)
            vt_ref[0, 0, base + DIFF_V_DIM:base + ATT_V_ROWS, :] = ones


def _c_in(x, g, w, qg, kg, cos, sin, seq):
    t, d = x.shape
    tk = _tile(seq, ATT_TK)
    tm = _tile(tk, C_IN_TM)
    per = tk // tm
    npos = seq // tm
    b = t // seq
    tok = lambda i: (i, 0)
    pos = lambda i: (i % npos, 0)
    vrows = DIFF_HEADS * ATT_V_ROWS
    return pl.pallas_call(
        _c_in_kernel,
        grid=(t // tm,),
        in_specs=[
            pl.BlockSpec((tm, d), tok),
            _resident((1, d)),
            _resident(w.shape),
            _resident(qg.shape),
            _resident(kg.shape),
            pl.BlockSpec((tm, LANES), pos),
            pl.BlockSpec((tm, LANES), pos),
        ],
        out_specs=[
            pl.BlockSpec((1, DIFF_TOTAL, tm), lambda i: (i // npos, 0, i % npos)),
            pl.BlockSpec((1, DIFF_HEADS, tm, DIFF_V_DIM), lambda i: (i // npos, 0, i % npos, 0)),
            pl.BlockSpec((1, 1, vrows, tm), lambda i: (i // npos, i % npos // per, 0, i % npos % per)),
        ],
        out_shape=[
            jax.ShapeDtypeStruct((b, DIFF_TOTAL, seq), BF16),
            jax.ShapeDtypeStruct((b, DIFF_HEADS, seq, DIFF_V_DIM), BF16),
            jax.ShapeDtypeStruct((b, seq // tk, vrows, tk), BF16),
        ],
        compiler_params=_params("parallel"),
        name="c_in",
    )(x, g, w, qg, kg, cos, sin)


def _attn_queries(qt_ref):
    qt = qt_ref[0].astype(F32)
    row = lax.broadcasted_iota(jnp.int32, qt.shape, 0)
    return jnp.concatenate([jnp.where(row < DIFF_QK_DIM, qt, 0.0), jnp.where(row >= DIFF_QK_DIM, qt, 0.0)],
                           axis=1).astype(BF16)


def _attn_finish(lam_ref, sg_ref, num, den, o_ref, *, tq, lam_init):
    lam = lam_ref[...]
    lam_full = (jnp.exp(jnp.sum(lam[0:1] * lam[1:2], axis=1, keepdims=True))
                - jnp.exp(jnp.sum(lam[2:3] * lam[3:4], axis=1, keepdims=True)) + lam_init)
    out = num / den
    o = out[:, :tq] - lam_full * out[:, tq:]
    o = o * lax.rsqrt(jnp.mean(o * o, axis=0, keepdims=True) + EPS)
    o_ref[0] = (o.T * sg_ref[...] * (1.0 - lam_init)).astype(BF16)


def _attn_bounded_kernel(lam_ref, sg_ref, qt_ref, k_ref, vt_ref, o_ref, acc_ref, *, tq, tk, nk, lam_init):
    qs = _attn_queries(qt_ref)
    acc_ref[...] = jnp.zeros_like(acc_ref)

    def body(j, carry):
        kj = k_ref[0, 0, pl.ds(pl.multiple_of(j * tk, tk), tk), :]
        acc_ref[...] += _mm(vt_ref[0, j], jnp.exp2(_mm(kj, qs)).astype(BF16))
        return carry

    lax.fori_loop(0, nk, body, 0, unroll=16)
    acc = acc_ref[...]
    _attn_finish(lam_ref, sg_ref, acc[:DIFF_V_DIM], acc[DIFF_V_DIM:DIFF_V_DIM + 1], o_ref, tq=tq, lam_init=lam_init)


def _attn_online_kernel(lam_ref, sg_ref, qt_ref, k_ref, vt_ref, o_ref, acc_ref, s_ref, *, tq, tk, nk, lam_init):
    qs = _attn_queries(qt_ref)
    acc_ref[...] = jnp.zeros_like(acc_ref)

    def scores(j, slot):
        s = _mm(k_ref[0, 0, pl.ds(pl.multiple_of(j * tk, tk), tk), :], qs)
        s_ref[slot] = s
        return jnp.max(s, axis=0, keepdims=True)

    def step(j, slot, carry, prefetch=True):
        m_prev, tile_max = carry
        next_max = scores(j + 1, 1 - slot) if prefetch else tile_max
        m_new = jnp.maximum(m_prev, tile_max)
        p = jnp.exp2(s_ref[slot] - m_new).astype(BF16)
        acc_ref[...] = jnp.exp2(m_prev - m_new) * acc_ref[...] + _mm(vt_ref[0, j], p)
        return m_new, next_max

    def pair(jj, carry):
        return step(2 * jj + 1, 1, step(2 * jj, 0, carry))

    carry = (jnp.full((1, 2 * tq), -jnp.inf, F32), scores(0, 0))
    carry = lax.fori_loop(0, nk // 2 - 1, pair, carry)
    step(nk - 1, 1, step(nk - 2, 0, carry), prefetch=False)
    acc = acc_ref[...]
    _attn_finish(lam_ref, sg_ref, acc[:DIFF_V_DIM], acc[DIFF_V_DIM:DIFF_V_DIM + 1], o_ref, tq=tq, lam_init=lam_init)


def _diff_attn(lam, sg, qt, k, vt, lam_init, score_bound):
    b, _, s, _ = k.shape
    d = DIFF_TOTAL
    nk, tk = vt.shape[1], vt.shape[3]
    assert nk % 2 == 0, "key tiles are consumed in pairs"
    tq = _tile(s, ATT_TQ)

    def call(body, name, scratch):
        return pl.pallas_call(
            functools.partial(body, tq=tq, tk=tk, nk=nk, lam_init=lam_init),
            grid=(b, DIFF_HEADS, s // tq),
            in_specs=[
                _resident(lam.shape),
                _resident(sg.shape),
                pl.BlockSpec((1, DIFF_V_DIM, tq), lambda bi, h, qi: (bi, h, qi)),
                pl.BlockSpec((1, 1, s, DIFF_V_DIM), lambda bi, h, qi: (bi, h, 0, 0)),
                pl.BlockSpec((1, nk, ATT_V_ROWS, tk), lambda bi, h, qi: (bi, 0, h, 0)),
            ],
            out_specs=pl.BlockSpec((1, tq, DIFF_V_DIM), lambda bi, h, qi: (bi, qi, h)),
            out_shape=jax.ShapeDtypeStruct((b, s, d), BF16),
            scratch_shapes=scratch,
            compiler_params=_params("parallel", "parallel", "arbitrary"),
            name=name,
        )

    acc = pltpu.VMEM((ATT_V_ROWS, 2 * tq), F32)
    bounded = call(_attn_bounded_kernel, "diff_attn_bounded", [acc])
    online = call(_attn_online_kernel, "diff_attn_online", [acc, pltpu.VMEM((2, tk, 2 * tq), F32)])
    return lax.cond(score_bound <= ATT_SAFE_LOG2, bounded, online, lam, sg, qt, k, vt)


def _rope_tables(seq, head_dim):
    half = head_dim // 2
    inv = ROPE_THETA ** (-jnp.arange(half, dtype=F32) / half)
    ang = jnp.arange(seq, dtype=F32)[:, None] * inv[None, :]
    cos = jnp.cos(ang)
    sin = jnp.sin(ang)
    reps = LANES // head_dim
    return (jnp.tile(jnp.concatenate([cos, cos], axis=-1), (1, reps)),
            jnp.tile(jnp.concatenate([-sin, sin], axis=-1), (1, reps)))


def _lambda_init(layer):
    return 0.8 - 0.6 * math.exp(-0.3 * layer)


def kernel(x_prompt, x_sample, norm_g, ffn_w_in, ffn_w_out, ab_w_in, ab_decay, ab_ret_norm_g, ab_conv_w, ab_conv_b,
           ab_conv_norm_g, ab_w_out, c_w_in, c_q_norm_g, c_k_norm_g, c_lambda, c_subln_g, c_w_out):
    depth = norm_g.shape[0]
    d = norm_g.shape[-1]
    ffn_w_in_b = ffn_w_in.astype(BF16)
    ffn_w_out_b = ffn_w_out.astype(BF16)
    ab_w_in_b = ab_w_in.astype(BF16)
    ab_w_out_b = ab_w_out.astype(BF16)
    c_w_in_b = c_w_in.astype(BF16)
    c_w_out_b = c_w_out.astype(BF16)
    max_seq = max(x_prompt.shape[1], x_sample.shape[1])
    rope = {dim: _rope_tables(max_seq, dim) for dim in (RET_HEAD_DIM, DIFF_QK_DIM)}

    def trunk(x3):
        b, s, _ = x3.shape
        x = x3.reshape(b * s, d)
        seq3 = lambda a: a.reshape(b, s, a.shape[-1])
        flat = lambda a: a.reshape(b * s, a.shape[-1])
        for layer in range(depth):
            g = norm_g[layer][:, None, :]
            x = _ffn(x, [], None, g[0], ffn_w_in_b, ffn_w_out_b, layer, 0)
            i = layer // 2
            if layer % 2 == 0:
                cos, sin = rope[RET_HEAD_DIM]
                q, k, v, gate, u = _ab_in(x, g[1], ab_w_in_b[i], cos, sin, s)
                mixed = _ab_mix(ab_decay[i], seq3(q), seq3(k), seq3(v), seq3(gate), ab_ret_norm_g[i][None],
                                seq3(u), ab_conv_w[i], ab_conv_b[i][None], ab_conv_norm_g[i][None])
                acts, w_pre = [flat(mixed)], ab_w_out_b[i]
            else:
                cos, sin = rope[DIFF_QK_DIM]
                reps = LANES // DIFF_QK_DIM
                qt, k, vt = _c_in(x, g[1], c_w_in_b[i], jnp.tile(c_q_norm_g[i], reps)[None],
                                  jnp.tile(c_k_norm_g[i], reps)[None], cos, sin, s)
                bound = (DIFF_QK_DIM * Q_SCALE_LOG2 * (1 + 2.0 ** -7) * jnp.max(jnp.abs(c_q_norm_g[i]))
                         * jnp.max(jnp.abs(c_k_norm_g[i])))
                o = _diff_attn(c_lambda[i], c_subln_g[i][None], qt, k, vt, _lambda_init(layer), bound)
                acts, w_pre = [flat(o)], c_w_out_b[i]
            x = _ffn(x, acts, w_pre, g[2], ffn_w_in_b, ffn_w_out_b, layer, 1)
        return x.reshape(b, s, d)

    return trunk(x_prompt), trunk(x_sample)
```
